```python
import jax, jax.numpy as jnp
from jax import lax
import numpy as np

D_MODEL = 2048
BATCH = 4
SEQ = 8192
DEPTH = 4

GRID_W = 64
CTX_LEN = 256
N_EVEN = (DEPTH + 1) // 2
N_ODD = DEPTH // 2

MIX_WIDTH = D_MODEL
RET_HEADS = 8
RET_DK = 128
RET_DV = 128
RET_QK_WIDTH = RET_HEADS * RET_DK
RET_WIDTH = RET_HEADS * RET_DV
RET_CHUNK = 128
CONV_WIDTH = MIX_WIDTH - RET_WIDTH
CONV_K = 31
EVEN_IN = 2 * RET_QK_WIDTH + 2 * RET_WIDTH + 2 * CONV_WIDTH
ROPE_AXES = (32, 48, 48)
ROPE_THETA = 10000.0

SGU_WIDTH = MIX_WIDTH
SGU_GROUPS = 8
SGU_CHUNK = 128

N_EXPERTS = 32
TOP_K = 4
D_EXPERT = 768
SWIGLU_LIMIT = 7.0
SWIGLU_ALPHA = 1.702
EPS = 1e-6

kernel_name = 'hybrid_retention_conv_sgu_moe_dit'


def _rms_norm(x, g):
    xf = x.astype(jnp.float32)
    y = xf * lax.rsqrt(jnp.mean(xf * xf, axis=-1, keepdims=True) + EPS)
    return (y * g.astype(jnp.float32)).astype(x.dtype)


def _layer_norm(x, g, b):
    xf = x.astype(jnp.float32)
    mu = jnp.mean(xf, axis=-1, keepdims=True)
    var = jnp.mean(jnp.square(xf - mu), axis=-1, keepdims=True)
    y = (xf - mu) * lax.rsqrt(var + EPS) * g.astype(jnp.float32) + b.astype(jnp.float32)
    return y.astype(x.dtype)


def _modulate(h, shift, scale):
    return h * (1 + scale) + shift


def _rope_angles(seq_pos, row_pos, col_pos):
    parts = []
    for pos, dim in zip((seq_pos, row_pos, col_pos), ROPE_AXES):
        inv = ROPE_THETA ** (-jnp.arange(0, dim, 2, dtype=jnp.float32) / dim)
        parts.append(pos.astype(jnp.float32)[:, None] * inv[None, :])
    return jnp.concatenate(parts, axis=-1)


def _apply_rope(t, ang):
    tp = t.reshape(t.shape[:-1] + (t.shape[-1] // 2, 2)).astype(jnp.float32)
    t0, t1 = tp[..., 0], tp[..., 1]
    cos = jnp.cos(ang)[None, :, None, :]
    sin = jnp.sin(ang)[None, :, None, :]
    out = jnp.stack([t0 * cos - t1 * sin, t0 * sin + t1 * cos], axis=-1)
    return out.reshape(t.shape).astype(t.dtype)


def _retention_scan(q, k, v, log_gamma, s0, include_diag):
    B, H, T, dk = q.shape
    dv = v.shape[-1]
    nc = T // RET_CHUNK
    qc = q.reshape(B, H, nc, RET_CHUNK, dk)
    kc = k.reshape(B, H, nc, RET_CHUNK, dk)
    vc = v.reshape(B, H, nc, RET_CHUNK, dv)
    idx = jnp.arange(RET_CHUNK, dtype=jnp.float32)
    rel = idx[:, None] - idx[None, :]
    mask = (rel >= 0) if include_diag else (rel > 0)
    decay = jnp.where(mask, jnp.exp(log_gamma[:, None, None] * jnp.maximum(rel, 0.0)), 0.0)
    scores = jnp.einsum('bhcnd,bhcmd->bhcnm', qc, kc) * decay[None, :, None]
    o_intra = jnp.einsum('bhcnm,bhcme->bhcne', scores, vc)
    zeta = jnp.exp(log_gamma[:, None] * (RET_CHUNK - 1 - idx))
    u = jnp.einsum('bhcmd,bhcme->bhcde', kc * zeta[None, :, None, :, None], vc)
    g_chunk = jnp.exp(log_gamma * RET_CHUNK)[None, :, None, None]

    def step(s, u_c):
        return g_chunk * s + u_c, s

    s_final, s_prev = lax.scan(step, s0, jnp.moveaxis(u, 2, 0))
    s_prev = jnp.moveaxis(s_prev, 0, 2)
    xi = jnp.exp(log_gamma[:, None] * (idx + 1.0))
    o_cross = jnp.einsum('bhcnd,bhcde->bhcne', qc, s_prev) * xi[None, :, None, :, None]
    return (o_intra + o_cross).reshape(B, H, T, dv), s_final


def _retention_bidir(q, k, v, lg_f, lg_b, s0_f, s0_b):
    o_f, s_f = _retention_scan(q, k, v, lg_f, s0_f, True)
    flip = lambda t: jnp.flip(t, axis=2)
    o_b, s_b = _retention_scan(flip(q), flip(k), flip(v), lg_b, s0_b, False)
    return o_f + flip(o_b), s_f, s_b


def _even_project(h, w_in, ang):
    B, T, _ = h.shape
    z = h @ w_in
    cuts = [RET_QK_WIDTH, 2 * RET_QK_WIDTH, 2 * RET_QK_WIDTH + RET_WIDTH,
            2 * RET_QK_WIDTH + 2 * RET_WIDTH, 2 * RET_QK_WIDTH + 2 * RET_WIDTH + CONV_WIDTH]
    q, k, v, g, a, b = jnp.split(z, cuts, axis=-1)
    q = _apply_rope(q.reshape(B, T, RET_HEADS, RET_DK), ang).transpose(0, 2, 1, 3)
    k = (_apply_rope(k.reshape(B, T, RET_HEADS, RET_DK), ang) * (RET_DK ** -0.5)).transpose(0, 2, 1, 3)
    v = v.reshape(B, T, RET_HEADS, RET_DV).transpose(0, 2, 1, 3)
    return q, k, v, g, a, b


def _depthwise_conv(x, w, b):
    y = lax.conv_general_dilated(
        x, w[:, None, :].astype(x.dtype), window_strides=(1,),
        padding=[(CONV_K // 2, CONV_K // 2)],
        dimension_numbers=('NWC', 'WIO', 'NWC'),
        feature_group_count=x.shape[-1])
    return y + b.astype(x.dtype)


def _even_output(o, g, a, b, gn_g, conv_w, conv_b, ln_g, ln_b, w_out):
    B, H, T, dv = o.shape
    dt = a.dtype
    of = o.astype(jnp.float32).transpose(0, 2, 1, 3)
    mu = jnp.mean(of, axis=-1, keepdims=True)
    var = jnp.mean(jnp.square(of - mu), axis=-1, keepdims=True)
    y = ((of - mu) * lax.rsqrt(var + EPS)).reshape(B, T, RET_WIDTH) * gn_g.astype(jnp.float32)
    ret = (jax.nn.silu(g.astype(jnp.float32)) * y).astype(dt)
    glu = a * jax.nn.sigmoid(b)
    cv = jax.nn.silu(_layer_norm(_depthwise_conv(glu, conv_w, conv_b), ln_g, ln_b))
    return jnp.concatenate([ret, cv], axis=-1) @ w_out


def _sgu_mixer(h, w_in, ln_g, ln_b, w_s, b_s, w_out):
    B, T, _ = h.shape
    z = jax.nn.gelu(h @ w_in, approximate=False)
    u, v = jnp.split(z, 2, axis=-1)
    v = _layer_norm(v, ln_g, ln_b)
    v = v.reshape(B, T // SGU_CHUNK, SGU_CHUNK, SGU_GROUPS, SGU_WIDTH // SGU_GROUPS)
    v = jnp.einsum('gpq,bcqgd->bcpgd', w_s.astype(v.dtype), v) + b_s.T.astype(v.dtype)[None, None, :, :, None]
    return (u * v.reshape(B, T, SGU_WIDTH)) @ w_out


def _moe(h, w_r, b_r, w_gu, b_gu, w_d, b_d):
    logits = h.astype(jnp.float32) @ w_r.astype(jnp.float32) + b_r.astype(jnp.float32)
    top_v, top_i = lax.top_k(logits, TOP_K)
    wts = jax.nn.softmax(top_v, axis=-1)
    gates = jnp.sum(jax.nn.one_hot(top_i, N_EXPERTS, dtype=jnp.float32) * wts[..., None], axis=-2)

    def expert(acc, p):
        wgu, bgu, wd, bd, gate = p
        z = h @ wgu + bgu
        zg = jnp.minimum(z[..., ::2], SWIGLU_LIMIT)
        zu = jnp.clip(z[..., 1::2], -SWIGLU_LIMIT, SWIGLU_LIMIT)
        act = (zu + 1) * (zg * jax.nn.sigmoid(SWIGLU_ALPHA * zg))
        y = act @ wd + bd
        return acc + gate[:, None] * y.astype(jnp.float32), None

    acc0 = jnp.zeros(h.shape, jnp.float32)
    out, _ = lax.scan(expert, acc0, (w_gu, b_gu, w_d, b_d, gates.T))
    return out.astype(h.dtype)


def setup_inputs(seed: int = 0) -> dict:
    key = jax.random.key(seed)
    keys = jax.random.split(key, 32)
    f32 = jnp.float32
    D = D_MODEL

    def nrm(i, shape, scale):
        return jax.random.normal(keys[i], shape, f32) * scale

    base = jnp.log(-jnp.log1p(-(2.0 ** (-5.0 - jnp.arange(RET_HEADS, dtype=f32)))))
    return {
        'x': nrm(0, (BATCH, SEQ, D), 1.0),
        'c': nrm(1, (BATCH, D), 1.0),
        'ctx': nrm(2, (BATCH, CTX_LEN, D), 1.0),
        'c_ctx': nrm(3, (D,), 1.0),
        'mod_w': nrm(4, (DEPTH, D, 6 * D), 0.5 * D ** -0.5),
        'mod_b': nrm(5, (DEPTH, 6 * D), 0.02),
        'norm_mix_g': 1.0 + nrm(6, (DEPTH, D), 0.02),
        'norm_ffn_g': 1.0 + nrm(7, (DEPTH, D), 0.02),
        'ev_w_in': nrm(8, (N_EVEN, D, EVEN_IN), D ** -0.5),
        'ev_w_out': nrm(9, (N_EVEN, MIX_WIDTH, D), MIX_WIDTH ** -0.5),
        'ret_decay_f': base[None, :] + nrm(10, (N_EVEN, RET_HEADS), 0.1),
        'ret_decay_b': base[None, :] + nrm(11, (N_EVEN, RET_HEADS), 0.1),
        'ret_gn_g': 1.0 + nrm(12, (N_EVEN, RET_WIDTH), 0.02),
        'conv_w': nrm(13, (N_EVEN, CONV_K, CONV_WIDTH), CONV_K ** -0.5),
        'conv_b': nrm(14, (N_EVEN, CONV_WIDTH), 0.02),
        'conv_ln_g': 1.0 + nrm(15, (N_EVEN, CONV_WIDTH), 0.02),
        'conv_ln_b': nrm(16, (N_EVEN, CONV_WIDTH), 0.02),
        'od_w_in': nrm(17, (N_ODD, D, 2 * SGU_WIDTH), D ** -0.5),
        'od_w_out': nrm(18, (N_ODD, SGU_WIDTH, D), SGU_WIDTH ** -0.5),
        'sgu_ln_g': 1.0 + nrm(19, (N_ODD, SGU_WIDTH), 0.02),
        'sgu_ln_b': nrm(20, (N_ODD, SGU_WIDTH), 0.02),
        'sgu_w': nrm(21, (N_ODD, SGU_GROUPS, SGU_CHUNK, SGU_CHUNK), SGU_CHUNK ** -0.5),
        'sgu_b': nrm(22, (N_ODD, SGU_GROUPS, SGU_CHUNK), 0.02),
        'router_w': nrm(23, (DEPTH, D, N_EXPERTS), D ** -0.5),
        'router_b': nrm(24, (DEPTH, N_EXPERTS), 0.01),
        'moe_w_gu': nrm(25, (DEPTH, N_EXPERTS, D, 2 * D_EXPERT), D ** -0.5),
        'moe_b_gu': nrm(26, (DEPTH, N_EXPERTS, 2 * D_EXPERT), 0.02),
        'moe_w_down': nrm(27, (DEPTH, N_EXPERTS, D_EXPERT, D), D_EXPERT ** -0.5),
        'moe_b_down': nrm(28, (DEPTH, N_EXPERTS, D), 0.02),
        'final_g': 1.0 + nrm(29, (D,), 0.02),
    }


def reference(x, c, ctx, c_ctx, mod_w, mod_b, norm_mix_g, norm_ffn_g, ev_w_in, ev_w_out,
              ret_decay_f, ret_decay_b, ret_gn_g, conv_w, conv_b, conv_ln_g, conv_ln_b,
              od_w_in, od_w_out, sgu_ln_g, sgu_ln_b, sgu_w, sgu_b, router_w, router_b,
              moe_w_gu, moe_b_gu, moe_w_down, moe_b_down, final_g):
    B, N, D = x.shape
    L = ctx.shape[1]
    rows = N // GRID_W
    lat_ang = _rope_angles(jnp.full((N,), L, jnp.int32),
                           jnp.repeat(jnp.arange(rows, dtype=jnp.int32), GRID_W),
                           jnp.tile(jnp.arange(GRID_W, dtype=jnp.int32), rows))
    zl = jnp.zeros((L,), jnp.int32)
    ctx_ang = _rope_angles(jnp.arange(L, dtype=jnp.int32), zl, zl)
    s_lat = jax.nn.silu(c)
    s_ctx = jax.nn.silu(c_ctx)[None, :]
    xc = ctx
    for layer in range(DEPTH):
        keep_ctx = layer < DEPTH - 1
        mod_l = (s_lat @ mod_w[layer] + mod_b[layer])[:, None, :]
        mod_c = (s_ctx @ mod_w[layer] + mod_b[layer])[:, None, :]
        sh_a, sc_a, g_a, sh_f, sc_f, g_f = jnp.split(mod_l, 6, axis=-1)
        csh_a, csc_a, cg_a, csh_f, csc_f, cg_f = jnp.split(mod_c, 6, axis=-1)

        h = _modulate(_rms_norm(x, norm_mix_g[layer]), sh_a, sc_a)
        hc = _modulate(_rms_norm(xc, norm_mix_g[layer]), csh_a, csc_a)
        if layer % 2 == 0:
            e = layer // 2
            lg_f = -jnp.exp(ret_decay_f[e].astype(jnp.float32))
            lg_b = -jnp.exp(ret_decay_b[e].astype(jnp.float32))
            q, k, v, gg, a, bb = _even_project(h, ev_w_in[e], lat_ang)
            qc, kc, vc, ggc, ac, bc = _even_project(hc, ev_w_in[e], ctx_ang)
            s0 = jnp.zeros((B, RET_HEADS, RET_DK, RET_DV), jnp.float32)
            oc, st_f, st_b = _retention_bidir(qc, kc, vc, lg_f, lg_b, s0, s0)
            o, _, _ = _retention_bidir(q, k, v, lg_f, lg_b, st_f, st_b)
            y = _even_output(o, gg, a, bb, ret_gn_g[e], conv_w[e], conv_b[e], conv_ln_g[e], conv_ln_b[e], ev_w_out[e])
            if keep_ctx:
                yc = _even_output(oc, ggc, ac, bc, ret_gn_g[e], conv_w[e], conv_b[e], conv_ln_g[e], conv_ln_b[e], ev_w_out[e])
        else:
            j = layer // 2
            y = _sgu_mixer(h, od_w_in[j], sgu_ln_g[j], sgu_ln_b[j], sgu_w[j], sgu_b[j], od_w_out[j])
            if keep_ctx:
                yc = _sgu_mixer(hc, od_w_in[j], sgu_ln_g[j], sgu_ln_b[j], sgu_w[j], sgu_b[j], od_w_out[j])
        x = x + g_a * y.astype(x.dtype)
        if keep_ctx:
            xc = xc + cg_a * yc.astype(xc.dtype)

        h = _modulate(_rms_norm(x, norm_ffn_g[layer]), sh_f, sc_f)
        moe_args = (router_w[layer], router_b[layer], moe_w_gu[layer], moe_b_gu[layer],
                    moe_w_down[layer], moe_b_down[layer])
        if keep_ctx:
            hc = _modulate(_rms_norm(xc, norm_ffn_g[layer]), csh_f, csc_f)
            tokens = jnp.concatenate([hc.reshape(B * L, D), h.reshape(B * N, D)], axis=0)
            out = _moe(tokens, *moe_args)
            xc = xc + cg_f * out[:B * L].reshape(B, L, D)
            y = out[B * L:].reshape(B, N, D)
        else:
            y = _moe(h.reshape(B * N, D), *moe_args).reshape(B, N, D)
        x = x + g_f * y
    return _rms_norm(x, final_g)
```

```python
import functools
import math

import jax
import jax.numpy as jnp
from jax import lax
from jax.experimental import pallas as pl
from jax.experimental.pallas import tpu as pltpu

F32 = jnp.float32
BF16 = jnp.bfloat16

LANES = 128
RET_HEADS = 8
RET_DK = 128
RET_CHUNK = 128
ROPE_AXES = (32, 48, 48)
ROPE_THETA = 10000.0
GRID_W = 64
SGU_GROUPS = 8
SGU_CHUNK = 128
TOP_K = 4
SWIGLU_LIMIT = 7.0
SWIGLU_ALPHA = 1.702
EPS = 1e-6
MOD_GROUPS = 8
HALO = 16
VMEM_CAP = 58 * 2 ** 20


def _cparams(sem, vmem_mb):
    return pltpu.CompilerParams(dimension_semantics=sem,
                                vmem_limit_bytes=min(int(vmem_mb * 2 ** 20), VMEM_CAP))


def _pick_tile(n_lat, n_ctx, cands):
    for t in cands:
        if n_lat % t == 0 and n_ctx % t == 0:
            return t
    raise ValueError("no token tile divides the latent and context lengths")


def _group_of(i, blocks_per_batch, nb):
    return jnp.minimum(i // blocks_per_batch, nb)


def _norm_mod(x, g, shift, scale):
    ms = jnp.mean(x * x, axis=-1, keepdims=True)
    return (x * lax.rsqrt(ms + EPS) * g) * (1.0 + scale) + shift


def _mod_kernel(c_ref, w_ref, b_ref, o_ref):
    c = c_ref[...]
    s = (c * jax.nn.sigmoid(c)).astype(BF16)
    o_ref[0] = jnp.dot(s, w_ref[0].astype(BF16), preferred_element_type=F32) + b_ref[0]


def _mod_table(cc, mod_w, mod_b):
    depth, d, n6 = mod_w.shape
    tn = min(d, 1024)
    return pl.pallas_call(
        _mod_kernel,
        grid=(depth, n6 // tn),
        in_specs=[pl.BlockSpec((MOD_GROUPS, d), lambda l, j: (0, 0)),
                  pl.BlockSpec((1, d, tn), lambda l, j: (l, 0, j)),
                  pl.BlockSpec((1, 1, tn), lambda l, j: (l, 0, j))],
        out_specs=pl.BlockSpec((1, MOD_GROUPS, tn), lambda l, j: (l, 0, j)),
        out_shape=jax.ShapeDtypeStruct((depth, MOD_GROUPS, n6), F32),
        compiler_params=_cparams(("arbitrary", "arbitrary"), 2 * d * tn * 4 / 2 ** 20 + 8),
        name="mod_table",
    )(cc, mod_w, mod_b.reshape(depth, 1, n6))


def _premix_even_kernel(x_ref, mod_ref, g_ref, w_ref, cos_ref, sin_ref, o_ref, h_scr, *, kscale):
    j = pl.program_id(1)

    @pl.when(j == 0)
    def _():
        m = mod_ref[0]
        h_scr[...] = _norm_mod(x_ref[...], g_ref[...], m[0:1], m[1:2]).astype(BF16)

    z = jnp.dot(h_scr[...], w_ref[...], preferred_element_type=F32)

    @pl.when(j < 2)
    def _():
        c = cos_ref[...]
        s = sin_ref[...]
        scale = jnp.where(j == 1, kscale, 1.0).astype(F32)
        for hh in range(z.shape[1] // RET_DK):
            t = z[:, hh * RET_DK:(hh + 1) * RET_DK]
            r = pltpu.roll(t, RET_DK // 2, axis=1)
            o_ref[:, hh * RET_DK:(hh + 1) * RET_DK] = ((t * c + r * s) * scale).astype(BF16)

    @pl.when(j >= 2)
    def _():
        o_ref[...] = z.astype(BF16)


def _premix_even(xs, mod_l, g, w_bf, cos_t, sin_t, nb, n_lat, n_ctx_total):
    nt, d = xs.shape
    n_out = w_bf.shape[1]
    tm = _pick_tile(n_lat, n_ctx_total, (1024, 512, 256))
    tn = RET_HEADS * RET_DK
    bpb = n_lat // tm
    n_lat_blocks = nb * bpb

    def tab_idx(i, j):
        return (jnp.where(i < n_lat_blocks, i % bpb, bpb + (i - n_lat_blocks)), 0)

    return pl.pallas_call(
        functools.partial(_premix_even_kernel, kscale=RET_DK ** -0.5),
        grid=(nt // tm, n_out // tn),
        in_specs=[pl.BlockSpec((tm, d), lambda i, j: (i, 0)),
                  pl.BlockSpec((1, 6, d), lambda i, j: (_group_of(i, bpb, nb), 0, 0)),
                  pl.BlockSpec((1, d), lambda i, j: (0, 0)),
                  pl.BlockSpec((d, tn), lambda i, j: (0, j)),
                  pl.BlockSpec((tm, RET_DK), tab_idx),
                  pl.BlockSpec((tm, RET_DK), tab_idx)],
        out_specs=pl.BlockSpec((tm, tn), lambda i, j: (i, j)),
        out_shape=jax.ShapeDtypeStruct((nt, n_out), BF16),
        scratch_shapes=[pltpu.VMEM((tm, d), BF16)],
        compiler_params=_cparams(("arbitrary", "arbitrary"),
                                 (2 * tm * d * 4 + tm * d * 2 + 2 * d * tn * 2 + 2 * tm * tn * 2
                                  + 2 * tm * tn * 4) / 2 ** 20 + 8),
        name="premix_even",
    )(xs, mod_l, g.reshape(1, d), w_bf, cos_t, sin_t)


def _retention_kernel(lg_ref, q_ref, k_ref, v_ref, g_ref, gn_ref, s0_ref, ret_in_ref, ret_ref, sfin_ref,
                      sb_scr, *, nc):
    del ret_in_ref
    hh = pl.program_id(1)
    lgf = lg_ref[0, hh]
    lgb = lg_ref[1, hh]
    c = RET_CHUNK
    ri = lax.broadcasted_iota(jnp.int32, (c, c), 0).astype(F32)
    ci = lax.broadcasted_iota(jnp.int32, (c, c), 1).astype(F32)
    rel = ri - ci
    dmat = jnp.where(rel >= 0, jnp.exp(lgf * jnp.maximum(rel, 0.0)), jnp.exp(lgb * jnp.maximum(-rel, 0.0)))
    xi_f = jnp.exp(lgf * (ri + 1.0))
    xi_b = jnp.exp(lgb * (c - ri))
    zeta_f = jnp.exp(lgf * (c - 1.0 - ri))
    zeta_b = jnp.exp(lgb * ri)
    gc_f = jnp.exp(lgf * c + jnp.zeros((c, c), F32))
    gc_b = jnp.exp(lgb * c + jnp.zeros((c, c), F32))
    tdims = (((0,), (0,)), ((), ()))

    def bwd_body(i, s):
        cc = nc - 1 - i
        r0 = pl.multiple_of(cc * c, c)
        sb_scr[cc] = s
        kz = (k_ref[pl.ds(r0, c), :].astype(F32) * zeta_b).astype(BF16)
        u = lax.dot_general(kz, v_ref[pl.ds(r0, c), :], tdims, preferred_element_type=F32)
        return gc_b * s + u

    sfin_ref[0, 0, 1] = lax.fori_loop(0, nc, bwd_body, s0_ref[0, 0, 1])

    def fwd_body(cc, s):
        r0 = pl.multiple_of(cc * c, c)
        q = q_ref[pl.ds(r0, c), :]
        k = k_ref[pl.ds(r0, c), :]
        v = v_ref[pl.ds(r0, c), :]
        a = lax.dot_general(q, k, (((1,), (1,)), ((), ())), preferred_element_type=F32)
        o = jnp.dot((a * dmat).astype(BF16), v, preferred_element_type=F32)
        o = o + jnp.dot(q, s.astype(BF16), preferred_element_type=F32) * xi_f
        o = o + jnp.dot(q, sb_scr[cc].astype(BF16), preferred_element_type=F32) * xi_b
        mu = jnp.mean(o, axis=-1, keepdims=True)
        dlt = o - mu
        var = jnp.mean(dlt * dlt, axis=-1, keepdims=True)
        y = dlt * lax.rsqrt(var + EPS) * gn_ref[...]
        g = g_ref[pl.ds(r0, c), :].astype(F32)
        ret_ref[pl.ds(r0, c), :] = (g * jax.nn.sigmoid(g) * y).astype(BF16)
        kz = (k.astype(F32) * zeta_f).astype(BF16)
        u = lax.dot_general(kz, v, tdims, preferred_element_type=F32)
        return gc_f * s + u

    sfin_ref[0, 0, 0] = lax.fori_loop(0, nc, fwd_body, s0_ref[0, 0, 0])


def _retention(z, lg, gn, s0, ret_prev, nb, seq, row0):
    nt = z.shape[0]
    nc = seq // RET_CHUNK
    rb0 = row0 // seq
    hcols = RET_HEADS

    def col(off):
        return lambda b, h: (rb0 + b, off * hcols + h)

    ret, sfin = pl.pallas_call(
        functools.partial(_retention_kernel, nc=nc),
        grid=(nb, RET_HEADS),
        in_specs=[pl.BlockSpec(memory_space=pltpu.SMEM),
                  pl.BlockSpec((seq, RET_DK), col(0)),
                  pl.BlockSpec((seq, RET_DK), col(1)),
                  pl.BlockSpec((seq, RET_DK), col(2)),
                  pl.BlockSpec((seq, RET_DK), col(3)),
                  pl.BlockSpec((1, RET_DK), lambda b, h: (0, h)),
                  pl.BlockSpec((1, 1, 2, RET_DK, RET_DK), lambda b, h: (b, h, 0, 0, 0)),
                  pl.BlockSpec(memory_space=pl.ANY)],
        out_specs=[pl.BlockSpec((seq, RET_DK), lambda b, h: (rb0 + b, h)),
                   pl.BlockSpec((1, 1, 2, RET_DK, RET_DK), lambda b, h: (b, h, 0, 0, 0))],
        out_shape=[jax.ShapeDtypeStruct((nt, RET_HEADS * RET_DK), BF16),
                   jax.ShapeDtypeStruct((nb, RET_HEADS, 2, RET_DK, RET_DK), F32)],
        scratch_shapes=[pltpu.VMEM((nc, RET_DK, RET_DK), F32)],
        input_output_aliases={7: 0},
        compiler_params=_cparams(("arbitrary", "arbitrary"),
                                 (10 * seq * RET_DK * 2 + nc * RET_DK * RET_DK * 4) / 2 ** 20 + 8),
        name="retention",
    )(lg, z, z, z, z, gn, s0, ret_prev)
    return ret, sfin


def _conv_kernel(a_ref, b_ref, ap_ref, bp_ref, an_ref, bn_ref, w_ref, cb_ref, lg_ref, lb_ref, o_ref, ext,
                 *, tm, n_lat_total, n_lat, n_ctx, ktaps):
    i = pl.program_id(0)
    row0 = i * tm
    is_lat = row0 < n_lat_total
    pos = jnp.where(is_lat, row0 % n_lat, (row0 - n_lat_total) % n_ctx)
    seq = jnp.where(is_lat, n_lat, n_ctx)
    keep_prev = (pos != 0).astype(F32)
    keep_next = (pos + tm != seq).astype(F32)

    def glu(a, b):
        return a.astype(F32) * jax.nn.sigmoid(b.astype(F32))

    ext[0:HALO, :] = glu(ap_ref[...], bp_ref[...]) * keep_prev
    ext[HALO:HALO + tm, :] = glu(a_ref[...], b_ref[...])
    ext[HALO + tm:HALO + tm + HALO, :] = glu(an_ref[...], bn_ref[...]) * keep_next

    half = ktaps // 2
    rc = 32
    for r in range(tm // rc):
        acc = jnp.zeros((rc, a_ref.shape[1]), F32)
        for j in range(ktaps):
            s = HALO + r * rc + j - half
            acc = acc + w_ref[j:j + 1, :] * ext[s:s + rc, :]
        y = acc + cb_ref[...]
        mu = jnp.mean(y, axis=-1, keepdims=True)
        dlt = y - mu
        var = jnp.mean(dlt * dlt, axis=-1, keepdims=True)
        yn = dlt * lax.rsqrt(var + EPS) * lg_ref[...] + lb_ref[...]
        o_ref[r * rc:(r + 1) * rc, :] = (yn * jax.nn.sigmoid(yn)).astype(BF16)


def _conv_module(z, conv_w, conv_b, ln_g, ln_b, nb, n_lat, n_ctx):
    nt = z.shape[0]
    ktaps, cw = conv_w.shape
    tm = 256
    assert n_lat % tm == 0 and n_ctx % tm == 0 and ktaps // 2 <= HALO
    a_blk = (2 * RET_HEADS * RET_DK + 2 * RET_HEADS * RET_DK) // cw
    hb = tm // HALO
    last = nt // HALO - 1
    kern = functools.partial(_conv_kernel, tm=tm, n_lat_total=nb * n_lat, n_lat=n_lat, n_ctx=n_ctx, ktaps=ktaps)
    vec = lambda i: (0, 0)
    return pl.pallas_call(
        kern,
        grid=(nt // tm,),
        in_specs=[pl.BlockSpec((tm, cw), lambda i: (i, a_blk)),
                  pl.BlockSpec((tm, cw), lambda i: (i, a_blk + 1)),
                  pl.BlockSpec((HALO, cw), lambda i: (jnp.maximum(i * hb - 1, 0), a_blk)),
                  pl.BlockSpec((HALO, cw), lambda i: (jnp.maximum(i * hb - 1, 0), a_blk + 1)),
                  pl.BlockSpec((HALO, cw), lambda i: (jnp.minimum((i + 1) * hb, last), a_blk)),
                  pl.BlockSpec((HALO, cw), lambda i: (jnp.minimum((i + 1) * hb, last), a_blk + 1)),
                  pl.BlockSpec((ktaps, cw), vec),
                  pl.BlockSpec((1, cw), vec),
                  pl.BlockSpec((1, cw), vec),
                  pl.BlockSpec((1, cw), vec)],
        out_specs=pl.BlockSpec((tm, cw), lambda i: (i, 0)),
        out_shape=jax.ShapeDtypeStruct((nt, cw), BF16),
        scratch_shapes=[pltpu.VMEM((tm + 2 * HALO, cw), F32)],
        compiler_params=_cparams(("arbitrary",), 40),
        name="conv_module",
    )(z, z, z, z, z, z, conv_w, conv_b.reshape(1, cw), ln_g.reshape(1, cw), ln_b.reshape(1, cw))


def _outproj_even_kernel(ret_ref, cv_ref, w_ref, x_ref, mod_ref, o_ref):
    rw = ret_ref.shape[1]
    y = jnp.dot(ret_ref[...], w_ref[0:rw, :], preferred_element_type=F32)
    y = y + jnp.dot(cv_ref[...], w_ref[rw:, :], preferred_element_type=F32)
    o_ref[...] = x_ref[...] + mod_ref[0][2:3] * y


def _outproj_even(ret, cv, w_bf, xs, mod_l, nb, n_lat, n_ctx_total):
    nt, d = xs.shape
    tm = _pick_tile(n_lat, n_ctx_total, (512, 256))
    bpb = n_lat // tm
    rw, cw = ret.shape[1], cv.shape[1]
    return pl.pallas_call(
        _outproj_even_kernel,
        grid=(nt // tm,),
        in_specs=[pl.BlockSpec((tm, rw), lambda i: (i, 0)),
                  pl.BlockSpec((tm, cw), lambda i: (i, 0)),
                  pl.BlockSpec((rw + cw, d), lambda i: (0, 0)),
                  pl.BlockSpec((tm, d), lambda i: (i, 0)),
                  pl.BlockSpec((1, 6, d), lambda i: (_group_of(i, bpb, nb), 0, 0))],
        out_specs=pl.BlockSpec((tm, d), lambda i: (i, 0)),
        out_shape=jax.ShapeDtypeStruct((nt, d), F32),
        input_output_aliases={3: 0},
        compiler_params=_cparams(("arbitrary",),
                                 (2 * (rw + cw) * d * 2 + 4 * tm * d * 4 + 2 * tm * (rw + cw) * 2
                                  + tm * d * 4) / 2 ** 20 + 8),
        name="outproj_even",
    )(ret, cv, w_bf, xs, mod_l)


def _premix_odd_kernel(x_ref, mod_ref, g_ref, w_ref, o_ref, h_scr):
    j = pl.program_id(1)

    @pl.when(j == 0)
    def _():
        m = mod_ref[0]
        h_scr[...] = _norm_mod(x_ref[...], g_ref[...], m[0:1], m[1:2]).astype(BF16)

    z = jnp.dot(h_scr[...], w_ref[...], preferred_element_type=F32)
    o_ref[...] = (0.5 * z * (1.0 + lax.erf(z * (2.0 ** -0.5)))).astype(BF16)


def _premix_odd(xs, mod_l, g, w_bf, nb, n_lat, nrows):
    d = xs.shape[1]
    n_out = w_bf.shape[1]
    tm = _pick_tile(n_lat, nrows, (1024, 512, 256))
    tn = 1024
    bpb = n_lat // tm
    return pl.pallas_call(
        _premix_odd_kernel,
        grid=(nrows // tm, n_out // tn),
        in_specs=[pl.BlockSpec((tm, d), lambda i, j: (i, 0)),
                  pl.BlockSpec((1, 6, d), lambda i, j: (_group_of(i, bpb, nb), 0, 0)),
                  pl.BlockSpec((1, d), lambda i, j: (0, 0)),
                  pl.BlockSpec((d, tn), lambda i, j: (0, j))],
        out_specs=pl.BlockSpec((tm, tn), lambda i, j: (i, j)),
        out_shape=jax.ShapeDtypeStruct((nrows, n_out), BF16),
        scratch_shapes=[pltpu.VMEM((tm, d), BF16)],
        compiler_params=_cparams(("arbitrary", "arbitrary"),
                                 (2 * tm * d * 4 + tm * d * 2 + 2 * d * tn * 2 + 2 * tm * tn * 2
                                  + 2 * tm * tn * 4) / 2 ** 20 + 8),
        name="premix_odd",
    )(xs, mod_l, g.reshape(1, d), w_bf)


def _sgu_out_kernel(u_ref, v_ref, lg_ref, lb_ref, ws_ref, bs_ref, w_ref, x_ref, mod_ref, o_ref, p_scr):
    tm, sw = u_ref.shape
    gw = sw // SGU_GROUPS
    v = v_ref[...].astype(F32)
    mu = jnp.mean(v, axis=-1, keepdims=True)
    dlt = v - mu
    var = jnp.mean(dlt * dlt, axis=-1, keepdims=True)
    p_scr[...] = (dlt * lax.rsqrt(var + EPS) * lg_ref[...] + lb_ref[...]).astype(BF16)
    for cc in range(tm // SGU_CHUNK):
        rows = slice(cc * SGU_CHUNK, (cc + 1) * SGU_CHUNK)
        for gi in range(SGU_GROUPS):
            cols = slice(gi * gw, (gi + 1) * gw)
            mixed = jnp.dot(ws_ref[gi], p_scr[rows, cols], preferred_element_type=F32) + bs_ref[:, cols]
            p_scr[rows, cols] = (u_ref[rows, cols].astype(F32) * mixed).astype(BF16)
    y = jnp.dot(p_scr[...], w_ref[...], preferred_element_type=F32)
    o_ref[...] = x_ref[...] + mod_ref[0][2:3] * y


def _sgu_out(uv, ln_g, ln_b, ws_bf, bias_full, w_bf, xs, mod_l, nb, n_lat, nrows):
    nt, d = xs.shape
    sw = w_bf.shape[0]
    tm = _pick_tile(n_lat, nrows, (512, 256))
    bpb = n_lat // tm
    vec = lambda i: (0, 0)
    return pl.pallas_call(
        _sgu_out_kernel,
        grid=(nrows // tm,),
        in_specs=[pl.BlockSpec((tm, sw), lambda i: (i, 0)),
                  pl.BlockSpec((tm, sw), lambda i: (i, 1)),
                  pl.BlockSpec((1, sw), vec),
                  pl.BlockSpec((1, sw), vec),
                  pl.BlockSpec((SGU_GROUPS, SGU_CHUNK, SGU_CHUNK), lambda i: (0, 0, 0)),
                  pl.BlockSpec((SGU_CHUNK, sw), vec),
                  pl.BlockSpec((sw, d), vec),
                  pl.BlockSpec((tm, d), lambda i: (i, 0)),
                  pl.BlockSpec((1, 6, d), lambda i: (_group_of(i, bpb, nb), 0, 0))],
        out_specs=pl.BlockSpec((tm, d), lambda i: (i, 0)),
        out_shape=jax.ShapeDtypeStruct((nt, d), F32),
        scratch_shapes=[pltpu.VMEM((tm, sw), BF16)],
        input_output_aliases={7: 0},
        compiler_params=_cparams(("arbitrary",),
                                 (2 * sw * d * 2 + 4 * tm * d * 4 + 4 * tm * sw * 2 + tm * sw * 2
                                  + 2 * SGU_CHUNK * sw * 4 + 2 * tm * sw * 4) / 2 ** 20 + 8),
        name="sgu_out",
    )(uv, uv, ln_g.reshape(1, sw), ln_b.reshape(1, sw), ws_bf, bias_full, w_bf, xs, mod_l)


def _router_kernel(x_ref, mod_ref, g_ref, whi_ref, wlo_ref, rb_ref, h_ref, aux_ref, cnt_ref, carry):
    i = pl.program_id(0)
    tm = x_ref.shape[0]

    @pl.when(i == 0)
    def _():
        carry[...] = jnp.zeros_like(carry)

    m = mod_ref[0]
    h = _norm_mod(x_ref[...], g_ref[...], m[3:4], m[4:5])
    h_ref[...] = h
    h_hi = h.astype(BF16)
    h_lo = (h - h_hi.astype(F32)).astype(BF16)
    logits = (jnp.dot(h_hi, whi_ref[...], preferred_element_type=F32)
              + jnp.dot(h_lo, whi_ref[...], preferred_element_type=F32)
              + jnp.dot(h_hi, wlo_ref[...], preferred_element_type=F32)) + rb_ref[...]

    lane = lax.broadcasted_iota(jnp.int32, (tm, LANES), 1).astype(F32)
    vals, idxs = [], []
    l = logits
    for _ in range(TOP_K):
        mx = jnp.max(l, axis=-1, keepdims=True)
        ix = jnp.min(jnp.where(l == mx, lane, float(LANES)), axis=-1, keepdims=True)
        vals.append(mx)
        idxs.append(ix)
        l = jnp.where(lane == ix, -jnp.inf, l)
    es = [jnp.exp(v - vals[0]) for v in vals]
    den = es[0]
    for e in es[1:]:
        den = den + e

    sel = jnp.zeros((tm, LANES), F32)
    for ix in idxs:
        sel = sel + jnp.where(lane == ix, 1.0, 0.0)
    rr = lax.broadcasted_iota(jnp.int32, (tm, tm), 0)
    rc = lax.broadcasted_iota(jnp.int32, (tm, tm), 1)
    ltri = jnp.where(rc < rr, 1.0, 0.0).astype(BF16)
    rank = jnp.dot(ltri, sel.astype(BF16), preferred_element_type=F32) + carry[...]
    carry[...] = carry[...] + jnp.sum(sel, axis=0, keepdims=True)
    cnt_ref[...] = carry[...]

    aux = jnp.zeros((tm, LANES), F32)
    for r in range(TOP_K):
        rk = jnp.sum(jnp.where(lane == idxs[r], rank, 0.0), axis=-1, keepdims=True)
        aux = jnp.where(lane == float(r), idxs[r], aux)
        aux = jnp.where(lane == float(TOP_K + r), rk, aux)
        aux = jnp.where(lane == float(2 * TOP_K + r), es[r] / den, aux)
    aux_ref[...] = aux


def _router(xs, mod_l, g, whi, wlo, rb, nb, n_lat, nrows):
    d = xs.shape[1]
    tm = 256
    bpb = n_lat // tm
    vec = lambda i: (0, 0)
    return pl.pallas_call(
        _router_kernel,
        grid=(nrows // tm,),
        in_specs=[pl.BlockSpec((tm, d), lambda i: (i, 0)),
                  pl.BlockSpec((1, 6, d), lambda i: (_group_of(i, bpb, nb), 0, 0)),
                  pl.BlockSpec((1, d), vec),
                  pl.BlockSpec((d, LANES), vec),
                  pl.BlockSpec((d, LANES), vec),
                  pl.BlockSpec((1, LANES), vec)],
        out_specs=[pl.BlockSpec((tm, d), lambda i: (i, 0)),
                   pl.BlockSpec((tm, LANES), lambda i: (i, 0)),
                   pl.BlockSpec((1, LANES), vec)],
        out_shape=[jax.ShapeDtypeStruct((nrows, d), F32),
                   jax.ShapeDtypeStruct((nrows, LANES), F32),
                   jax.ShapeDtypeStruct((1, LANES), F32)],
        scratch_shapes=[pltpu.VMEM((1, LANES), F32)],
        compiler_params=_cparams(("arbitrary",), 32),
        name="router",
    )(xs, mod_l, g.reshape(1, d), whi, wlo, rb)


def _dispatch_kernel(pos_ref, h_ref, hs_ref, sem):
    tm = h_ref.shape[0]

    def row_copy(t, p):
        return pltpu.make_async_copy(h_ref.at[pl.ds(t, 1), :], hs_ref.at[pl.ds(p, 1), :], sem)

    def issue(t, carry):
        for r in range(TOP_K):
            row_copy(t, pos_ref[0, 0, t * TOP_K + r]).start()
        return carry

    lax.fori_loop(0, tm, issue, 0)

    def drain(t, carry):
        for r in range(TOP_K):
            row_copy(t, pos_ref[0, 0, t * TOP_K + r]).wait()
        return carry

    lax.fori_loop(0, tm, drain, 0)


def _dispatch(h, pos):
    nrows, d = h.shape
    tm = 256
    pos3 = pos.reshape(nrows // tm, 1, tm * TOP_K)
    return pl.pallas_call(
        _dispatch_kernel,
        grid=(nrows // tm,),
        in_specs=[pl.BlockSpec((1, 1, tm * TOP_K), lambda i: (i, 0, 0), memory_space=pltpu.SMEM),
                  pl.BlockSpec((tm, d), lambda i: (i, 0))],
        out_specs=pl.BlockSpec(memory_space=pl.ANY),
        out_shape=jax.ShapeDtypeStruct((nrows * TOP_K, d), F32),
        scratch_shapes=[pltpu.SemaphoreType.DMA(())],
        compiler_params=_cparams(("arbitrary",), 16),
        name="moe_dispatch",
    )(pos3, h)


def _expert_kernel(tile_ref, exp_ref, lo_ref, hi_ref, first_ref, hs_ref, wgu_ref, bgu_ref, wd_ref, bd_ref, o_ref):
    del tile_ref, exp_ref
    w = pl.program_id(0)
    tm = hs_ref.shape[0]
    de = wd_ref.shape[1]
    z = jnp.dot(hs_ref[...].astype(BF16), wgu_ref[0], preferred_element_type=F32) + bgu_ref[0]
    zg = jnp.minimum(z[:, :de], SWIGLU_LIMIT)
    zu = jnp.clip(z[:, de:], -SWIGLU_LIMIT, SWIGLU_LIMIT)
    act = (zu + 1.0) * (zg * jax.nn.sigmoid(SWIGLU_ALPHA * zg))
    y = jnp.dot(act.astype(BF16), wd_ref[0], preferred_element_type=F32) + bd_ref[0]
    row = lax.broadcasted_iota(jnp.int32, (tm, 1), 0)
    mine = (row >= lo_ref[w]) & (row < hi_ref[w])

    @pl.when(first_ref[w] == 1)
    def _():
        o_ref[...] = jnp.where(mine, y, 0.0)

    @pl.when(first_ref[w] == 0)
    def _():
        o_ref[...] = jnp.where(mine, y, o_ref[...])


def _experts(hs, items, wgu_bf, bgu, wd_bf, bd, tm):
    p, d = hs.shape
    ne, _, n2 = wgu_bf.shape
    de = n2 // 2
    nw = items[0].shape[0]
    grid_spec = pltpu.PrefetchScalarGridSpec(
        num_scalar_prefetch=5,
        grid=(nw,),
        in_specs=[pl.BlockSpec((tm, d), lambda w, t, e, lo, hi, f: (t[w], 0)),
                  pl.BlockSpec((1, d, n2), lambda w, t, e, lo, hi, f: (e[w], 0, 0)),
                  pl.BlockSpec((1, 1, n2), lambda w, t, e, lo, hi, f: (e[w], 0, 0)),
                  pl.BlockSpec((1, de, d), lambda w, t, e, lo, hi, f: (e[w], 0, 0)),
                  pl.BlockSpec((1, 1, d), lambda w, t, e, lo, hi, f: (e[w], 0, 0))],
        out_specs=pl.BlockSpec((tm, d), lambda w, t, e, lo, hi, f: (t[w], 0)),
    )
    return pl.pallas_call(
        _expert_kernel,
        grid_spec=grid_spec,
        out_shape=jax.ShapeDtypeStruct((p, d), F32),
        compiler_params=_cparams(("arbitrary",),
                                 (4 * tm * d * 4 + 2 * d * n2 * 2 + 2 * de * d * 2 + 2 * tm * n2 * 4
                                  + tm * d * 4) / 2 ** 20 + 6),
        name="moe_experts",
    )(*items, hs, wgu_bf, bgu.reshape(ne, 1, n2), wd_bf, bd.reshape(ne, 1, d))


def _expert_items(counts, p, tm, ne):
    ntiles = p // tm
    nw = ntiles + ne - 1
    off = jnp.concatenate([jnp.zeros((1,), jnp.int32), jnp.cumsum(counts)])
    first_tile = off[:-1] // tm
    last_tile = (off[1:] - 1) // tm
    ntile_e = jnp.where(counts > 0, last_tile - first_tile + 1, 0)
    base_incl = jnp.cumsum(ntile_e)
    base = base_incl - ntile_e
    total = base_incl[-1]
    w = jnp.arange(nw, dtype=jnp.int32)
    wc = jnp.minimum(w, total - 1)
    e = jnp.searchsorted(base_incl, wc, side="right").astype(jnp.int32)
    t = first_tile[e] + (wc - base[e])
    lo = jnp.clip(off[e] - t * tm, 0, tm)
    hi = jnp.clip(off[e + 1] - t * tm, 0, tm)
    valid = w < total
    lo = jnp.where(valid, lo, 0)
    hi = jnp.where(valid, hi, 0)
    prev_t = jnp.concatenate([jnp.full((1,), -1, jnp.int32), t[:-1]])
    first = (valid & (t != prev_t)).astype(jnp.int32)
    return (t.astype(jnp.int32), e, lo.astype(jnp.int32), hi.astype(jnp.int32), first), off


def _combine_kernel(pos_ref, ys_ref, aux_ref, x_ref, mod_ref, fg_ref, o_ref, buf, sem, *, final):
    tm = x_ref.shape[0]

    def row_copy(t, r, p):
        return pltpu.make_async_copy(ys_ref.at[pl.ds(p, 1), :], buf.at[r, pl.ds(t, 1), :], sem)

    def issue(t, carry):
        for r in range(TOP_K):
            row_copy(t, r, pos_ref[0, 0, t * TOP_K + r]).start()
        return carry

    lax.fori_loop(0, tm, issue, 0)

    def drain(t, carry):
        for r in range(TOP_K):
            row_copy(t, r, pos_ref[0, 0, t * TOP_K + r]).wait()
        return carry

    lax.fori_loop(0, tm, drain, 0)

    aux = aux_ref[...]
    y = buf[0] * aux[:, 2 * TOP_K:2 * TOP_K + 1]
    for r in range(1, TOP_K):
        y = y + buf[r] * aux[:, 2 * TOP_K + r:2 * TOP_K + r + 1]
    xn = x_ref[...] + mod_ref[0][5:6] * y
    if final:
        ms = jnp.mean(xn * xn, axis=-1, keepdims=True)
        xn = xn * lax.rsqrt(ms + EPS) * fg_ref[...]
    o_ref[...] = xn


def _combine(ys, pos, aux, xs, mod_l, final_g, nb, n_lat, nrows, final):
    d = xs.shape[1]
    tm = 256
    bpb = n_lat // tm
    pos3 = pos.reshape(nrows // tm, 1, tm * TOP_K)
    out_rows = nrows if final else xs.shape[0]
    return pl.pallas_call(
        functools.partial(_combine_kernel, final=final),
        grid=(nrows // tm,),
        in_specs=[pl.BlockSpec((1, 1, tm * TOP_K), lambda i: (i, 0, 0), memory_space=pltpu.SMEM),
                  pl.BlockSpec(memory_space=pl.ANY),
                  pl.BlockSpec((tm, LANES), lambda i: (i, 0)),
                  pl.BlockSpec((tm, d), lambda i: (i, 0)),
                  pl.BlockSpec((1, 6, d), lambda i: (_group_of(i, bpb, nb), 0, 0)),
                  pl.BlockSpec((1, d), lambda i: (0, 0))],
        out_specs=pl.BlockSpec((tm, d), lambda i: (i, 0)),
        out_shape=jax.ShapeDtypeStruct((out_rows, d), F32),
        scratch_shapes=[pltpu.VMEM((TOP_K, tm, d), F32), pltpu.SemaphoreType.DMA(())],
        input_output_aliases={} if final else {3: 0},
        compiler_params=_cparams(("arbitrary",), (TOP_K * tm * d * 4 + 6 * tm * d * 4) / 2 ** 20 + 8),
        name="moe_combine",
    )(pos3, ys, aux, xs, mod_l, final_g.reshape(1, d))


def _moe(xs, mod_l, g, rw, rb, wgu, bgu, wd, bd, final_g, nb, n_lat, nrows, final):
    d = xs.shape[1]
    ne = rw.shape[1]
    de = wd.shape[1]
    rw_pad = jnp.zeros((d, LANES), F32).at[:, :ne].set(rw.astype(F32))
    whi = rw_pad.astype(BF16)
    wlo = (rw_pad - whi.astype(F32)).astype(BF16)
    rb_pad = jnp.full((1, LANES), -1e30, F32).at[0, :ne].set(rb.astype(F32))
    h, aux, cnt = _router(xs, mod_l, g, whi, wlo, rb_pad, nb, n_lat, nrows)

    counts = cnt[0, :ne].astype(jnp.int32)
    tm_e = 512
    p = nrows * TOP_K
    items, off = _expert_items(counts, p, tm_e, ne)
    idx = aux[:, 0:TOP_K].astype(jnp.int32)
    rank = aux[:, TOP_K:2 * TOP_K].astype(jnp.int32)
    pos = (off[idx] + rank).reshape(-1)

    hs = _dispatch(h, pos)
    wgu_bf = jnp.concatenate([wgu[:, :, 0::2], wgu[:, :, 1::2]], axis=-1).astype(BF16)
    bgu_p = jnp.concatenate([bgu[:, 0::2], bgu[:, 1::2]], axis=-1).astype(F32)
    ys = _experts(hs, items, wgu_bf, bgu_p, wd.astype(BF16), bd.astype(F32), tm_e)
    del de
    return _combine(ys, pos, aux, xs, mod_l, final_g, nb, n_lat, nrows, final)


def _rope_tables(nb, n_lat, n_ctx):
    def angles(seq_pos, row_pos, col_pos):
        parts = []
        for posv, dim in zip((seq_pos, row_pos, col_pos), ROPE_AXES):
            inv = ROPE_THETA ** (-jnp.arange(0, dim, 2, dtype=F32) / dim)
            parts.append(posv.astype(F32)[:, None] * inv[None, :])
        return jnp.concatenate(parts, axis=-1)

    rows = n_lat // GRID_W
    lat = angles(jnp.full((n_lat,), n_ctx, jnp.int32),
                 jnp.repeat(jnp.arange(rows, dtype=jnp.int32), GRID_W),
                 jnp.tile(jnp.arange(GRID_W, dtype=jnp.int32), rows))
    zl = jnp.zeros((n_ctx,), jnp.int32)
    ctx = angles(jnp.arange(n_ctx, dtype=jnp.int32), zl, zl)
    ang = jnp.concatenate([lat, jnp.tile(ctx, (nb, 1))], axis=0)
    cos, sin = jnp.cos(ang), jnp.sin(ang)
    return jnp.concatenate([cos, cos], axis=-1), jnp.concatenate([-sin, sin], axis=-1)


def _deinterleave_heads(w):
    d, n = w.shape
    w4 = w.reshape(d, n // RET_DK, RET_DK // 2, 2)
    return jnp.concatenate([w4[..., 0], w4[..., 1]], axis=-1).reshape(d, n)


def kernel(x, c, ctx, c_ctx, mod_w, mod_b, norm_mix_g, norm_ffn_g, ev_w_in, ev_w_out, ret_decay_f, ret_decay_b,
           ret_gn_g, conv_w, conv_b, conv_ln_g, conv_ln_b, od_w_in, od_w_out, sgu_ln_g, sgu_ln_b, sgu_w, sgu_b,
           router_w, router_b, moe_w_gu, moe_b_gu, moe_w_down, moe_b_down, final_g):
    nb, n_lat, d = x.shape
    n_ctx = ctx.shape[1]
    depth = mod_w.shape[0]
    n_lat_total = nb * n_lat
    n_ctx_total = nb * n_ctx
    nt = n_lat_total + n_ctx_total
    assert nb < MOD_GROUPS and n_lat % GRID_W == 0
    qk = RET_HEADS * RET_DK

    xs = jnp.concatenate([x.reshape(n_lat_total, d), ctx.reshape(n_ctx_total, d)], axis=0).astype(F32)
    cc = jnp.zeros((MOD_GROUPS, d), F32).at[:nb].set(c.astype(F32)).at[nb].set(c_ctx.astype(F32))
    mod = _mod_table(cc, mod_w, mod_b).reshape(depth, MOD_GROUPS, 6, d)
    cos_t, sin_t = _rope_tables(nb, n_lat, n_ctx)

    out = None
    for layer in range(depth):
        last = layer == depth - 1
        nrows = n_lat_total if last else nt
        mod_l = mod[layer]
        if layer % 2 == 0:
            e = layer // 2
            w_in = ev_w_in[e]
            w_in = jnp.concatenate([_deinterleave_heads(w_in[:, :qk]), _deinterleave_heads(w_in[:, qk:2 * qk]),
                                    w_in[:, 2 * qk:]], axis=1).astype(BF16)
            z = _premix_even(xs, mod_l, norm_mix_g[layer], w_in, cos_t, sin_t, nb, n_lat, n_ctx_total)
            lg = jnp.stack([-jnp.exp(ret_decay_f[e].astype(F32)), -jnp.exp(ret_decay_b[e].astype(F32))])
            gn = ret_gn_g[e].reshape(1, qk).astype(F32)
            s0 = jnp.zeros((nb, RET_HEADS, 2, RET_DK, RET_DK), F32)
            ret0 = jnp.zeros((nt, qk), BF16)
            ret, st = _retention(z, lg, gn, s0, ret0, nb, n_ctx, n_lat_total)
            ret, _ = _retention(z, lg, gn, st, ret, nb, n_lat, 0)
            cv = _conv_module(z, conv_w[e].astype(F32), conv_b[e].astype(F32), conv_ln_g[e].astype(F32),
                              conv_ln_b[e].astype(F32), nb, n_lat, n_ctx)
            xs = _outproj_even(ret, cv, ev_w_out[e].astype(BF16), xs, mod_l, nb, n_lat, n_ctx_total)
        else:
            j = layer // 2
            uv = _premix_odd(xs, mod_l, norm_mix_g[layer], od_w_in[j].astype(BF16), nb, n_lat, nrows)
            sw = od_w_out.shape[1]
            bias_full = jnp.repeat(sgu_b[j].astype(F32).T, sw // SGU_GROUPS, axis=1)
            xs = _sgu_out(uv, sgu_ln_g[j].astype(F32), sgu_ln_b[j].astype(F32), sgu_w[j].astype(BF16), bias_full,
                          od_w_out[j].astype(BF16), xs, mod_l, nb, n_lat, nrows)
        res = _moe(xs, mod_l, norm_ffn_g[layer], router_w[layer], router_b[layer], moe_w_gu[layer],
                   moe_b_gu[layer], moe_w_down[layer], moe_b_down[layer], final_g.astype(F32), nb, n_lat, nrows, last)
        if last:
            out = res
        else:
            xs = res
    return out.reshape(nb, n_lat, d).astype(x.dtype)
```

```python
import functools
import math

import jax
import jax.numpy as jnp
from jax import lax
from jax.experimental import pallas as pl
from jax.experimental.pallas import tpu as pltpu

F32 = jnp.float32
BF16 = jnp.bfloat16

LANES = 128
RET_HEADS = 8
RET_DK = 128
RET_CHUNK = 128
ROPE_AXES = (32, 48, 48)
ROPE_THETA = 10000.0
GRID_W = 64
SGU_GROUPS = 8
SGU_CHUNK = 128
TOP_K = 4
SWIGLU_LIMIT = 7.0
SWIGLU_ALPHA = 1.702
EPS = 1e-6
MOD_GROUPS = 8
HALO = 16
VMEM_CAP = 58 * 2 ** 20


def _cparams(sem, vmem_mb):
    return pltpu.CompilerParams(dimension_semantics=sem,
                                vmem_limit_bytes=min(int(vmem_mb * 2 ** 20), VMEM_CAP))


def _pick_tile(n_lat, n_ctx, cands):
    for t in cands:
        if n_lat % t == 0 and n_ctx % t == 0:
            return t
    raise ValueError("no token tile divides the latent and context lengths")


def _group_of(i, blocks_per_batch, nb):
    return jnp.minimum(i // blocks_per_batch, nb)


def _norm_mod(x, g, shift, scale):
    ms = jnp.mean(x * x, axis=-1, keepdims=True)
    return (x * lax.rsqrt(ms + EPS) * g) * (1.0 + scale) + shift


def _mod_kernel(c_ref, w_ref, b_ref, o_ref):
    c = c_ref[...]
    s = (c * jax.nn.sigmoid(c)).astype(BF16)
    o_ref[0] = jnp.dot(s, w_ref[0].astype(BF16), preferred_element_type=F32) + b_ref[0]


def _mod_table(cc, mod_w, mod_b):
    depth, d, n6 = mod_w.shape
    tn = min(d, 1024)
    return pl.pallas_call(
        _mod_kernel,
        grid=(depth, n6 // tn),
        in_specs=[pl.BlockSpec((MOD_GROUPS, d), lambda l, j: (0, 0)),
                  pl.BlockSpec((1, d, tn), lambda l, j: (l, 0, j)),
                  pl.BlockSpec((1, 1, tn), lambda l, j: (l, 0, j))],
        out_specs=pl.BlockSpec((1, MOD_GROUPS, tn), lambda l, j: (l, 0, j)),
        out_shape=jax.ShapeDtypeStruct((depth, MOD_GROUPS, n6), F32),
        compiler_params=_cparams(("arbitrary", "arbitrary"), 2 * d * tn * 4 / 2 ** 20 + 8),
        name="mod_table",
    )(cc, mod_w, mod_b.reshape(depth, 1, n6))


def _premix_even_kernel(x_ref, mod_ref, g_ref, w_ref, cos_ref, sin_ref, o_ref, h_scr, *, kscale):
    j = pl.program_id(1)

    @pl.when(j == 0)
    def _():
        m = mod_ref[0]
        h_scr[...] = _norm_mod(x_ref[...], g_ref[...], m[0:1], m[1:2]).astype(BF16)

    z = jnp.dot(h_scr[...], w_ref[...], preferred_element_type=F32)

    @pl.when(j < 2)
    def _():
        c = cos_ref[...]
        s = sin_ref[...]
        scale = jnp.where(j == 1, kscale, 1.0).astype(F32)
        for hh in range(z.shape[1] // RET_DK):
            t = z[:, hh * RET_DK:(hh + 1) * RET_DK]
            r = pltpu.roll(t, RET_DK // 2, axis=1)
            o_ref[:, hh * RET_DK:(hh + 1) * RET_DK] = ((t * c + r * s) * scale).astype(BF16)

    @pl.when(j >= 2)
    def _():
        o_ref[...] = z.astype(BF16)


def _premix_even(xs, mod_l, g, w_bf, cos_t, sin_t, nb, n_lat, n_ctx_total):
    nt, d = xs.shape
    n_out = w_bf.shape[1]
    tm = _pick_tile(n_lat, n_ctx_total, (1024, 512, 256))
    tn = RET_HEADS * RET_DK
    bpb = n_lat // tm
    n_lat_blocks = nb * bpb

    def tab_idx(i, j):
        return (jnp.where(i < n_lat_blocks, i % bpb, bpb + (i - n_lat_blocks)), 0)

    return pl.pallas_call(
        functools.partial(_premix_even_kernel, kscale=RET_DK ** -0.5),
        grid=(nt // tm, n_out // tn),
        in_specs=[pl.BlockSpec((tm, d), lambda i, j: (i, 0)),
                  pl.BlockSpec((1, 6, d), lambda i, j: (_group_of(i, bpb, nb), 0, 0)),
                  pl.BlockSpec((1, d), lambda i, j: (0, 0)),
                  pl.BlockSpec((d, tn), lambda i, j: (0, j)),
                  pl.BlockSpec((tm, RET_DK), tab_idx),
                  pl.BlockSpec((tm, RET_DK), tab_idx)],
        out_specs=pl.BlockSpec((tm, tn), lambda i, j: (i, j)),
        out_shape=jax.ShapeDtypeStruct((nt, n_out), BF16),
        scratch_shapes=[pltpu.VMEM((tm, d), BF16)],
        compiler_params=_cparams(("arbitrary", "arbitrary"),
                                 (2 * tm * d * 4 + tm * d * 2 + 2 * d * tn * 2 + 2 * tm * tn * 2
                                  + 2 * tm * tn * 4) / 2 ** 20 + 8),
        name="premix_even",
    )(xs, mod_l, g.reshape(1, d), w_bf, cos_t, sin_t)


def _retention_kernel(lg_ref, q_ref, k_ref, v_ref, g_ref, gn_ref, s0_ref, ret_in_ref, ret_ref, sfin_ref,
                      st_scr, *, nc, unroll):
    del ret_in_ref
    hh = pl.program_id(1)
    lgf = lg_ref[0, hh]
    lgb = lg_ref[1, hh]
    c = RET_CHUNK
    ri = lax.broadcasted_iota(jnp.int32, (c, c), 0).astype(F32)
    ci = lax.broadcasted_iota(jnp.int32, (c, c), 1).astype(F32)
    rel = ri - ci
    dmat = jnp.where(rel >= 0, jnp.exp(lgf * jnp.maximum(rel, 0.0)), jnp.exp(lgb * jnp.maximum(-rel, 0.0)))
    xi_f = jnp.exp(lgf * (ri + 1.0))
    xi_b = jnp.exp(lgb * (c - ri))
    zeta_f = jnp.exp(lgf * (c - 1.0 - ci))
    zeta_b = jnp.exp(lgb * ci)
    gc_f = jnp.exp(lgf * c + jnp.zeros((c, c), F32))
    gc_b = jnp.exp(lgb * c + jnp.zeros((c, c), F32))

    def local_sums(i, carry):
        for u in range(unroll):
            cc = i * unroll + u
            r0 = pl.multiple_of(cc * c, c)
            kt = k_ref[pl.ds(r0, c), :].astype(F32).T
            v = v_ref[pl.ds(r0, c), :]
            st_scr[cc, :, 0:c] = jnp.dot((kt * zeta_f).astype(BF16), v, preferred_element_type=F32)
            st_scr[cc, :, c:2 * c] = jnp.dot((kt * zeta_b).astype(BF16), v, preferred_element_type=F32)
        return carry

    lax.fori_loop(0, nc // unroll, local_sums, 0)

    def scans(i, carry):
        sf, sb = carry
        cb = nc - 1 - i
        uf = st_scr[i, :, 0:c]
        st_scr[i, :, 0:c] = sf
        ub = st_scr[cb, :, c:2 * c]
        st_scr[cb, :, c:2 * c] = sb
        return gc_f * sf + uf, gc_b * sb + ub

    sf, sb = lax.fori_loop(0, nc, scans, (s0_ref[0, 0, 0], s0_ref[0, 0, 1]))
    sfin_ref[0, 0, 0] = sf
    sfin_ref[0, 0, 1] = sb

    def outputs(i, carry):
        for u in range(unroll):
            cc = i * unroll + u
            r0 = pl.multiple_of(cc * c, c)
            q = q_ref[pl.ds(r0, c), :]
            a = lax.dot_general(q, k_ref[pl.ds(r0, c), :], (((1,), (1,)), ((), ())), preferred_element_type=F32)
            o = jnp.dot((a * dmat).astype(BF16), v_ref[pl.ds(r0, c), :], preferred_element_type=F32)
            cr = jnp.dot(q, st_scr[cc].astype(BF16), preferred_element_type=F32)
            o = o + cr[:, 0:c] * xi_f + cr[:, c:2 * c] * xi_b
            mu = jnp.mean(o, axis=-1, keepdims=True)
            dlt = o - mu
            var = jnp.mean(dlt * dlt, axis=-1, keepdims=True)
            y = dlt * lax.rsqrt(var + EPS) * gn_ref[...]
            g = g_ref[pl.ds(r0, c), :].astype(F32)
            ret_ref[pl.ds(r0, c), :] = (g * jax.nn.sigmoid(g) * y).astype(BF16)
        return carry

    lax.fori_loop(0, nc // unroll, outputs, 0)


def _retention(z, lg, gn, s0, ret_prev, nb, seq, row0):
    nt = z.shape[0]
    nc = seq // RET_CHUNK
    rb0 = row0 // seq
    hcols = RET_HEADS

    def col(off):
        return lambda b, h: (rb0 + b, off * hcols + h)

    ret, sfin = pl.pallas_call(
        functools.partial(_retention_kernel, nc=nc, unroll=max(u for u in (4, 2, 1) if nc % u == 0)),
        grid=(nb, RET_HEADS),
        in_specs=[pl.BlockSpec(memory_space=pltpu.SMEM),
                  pl.BlockSpec((seq, RET_DK), col(0)),
                  pl.BlockSpec((seq, RET_DK), col(1)),
                  pl.BlockSpec((seq, RET_DK), col(2)),
                  pl.BlockSpec((seq, RET_DK), col(3)),
                  pl.BlockSpec((1, RET_DK), lambda b, h: (0, h)),
                  pl.BlockSpec((1, 1, 2, RET_DK, RET_DK), lambda b, h: (b, h, 0, 0, 0)),
                  pl.BlockSpec(memory_space=pl.ANY)],
        out_specs=[pl.BlockSpec((seq, RET_DK), lambda b, h: (rb0 + b, h)),
                   pl.BlockSpec((1, 1, 2, RET_DK, RET_DK), lambda b, h: (b, h, 0, 0, 0))],
        out_shape=[jax.ShapeDtypeStruct((nt, RET_HEADS * RET_DK), BF16),
                   jax.ShapeDtypeStruct((nb, RET_HEADS, 2, RET_DK, RET_DK), F32)],
        scratch_shapes=[pltpu.VMEM((nc, RET_DK, 2 * RET_DK), F32)],
        input_output_aliases={7: 0},
        compiler_params=_cparams(("arbitrary", "arbitrary"),
                                 (10 * seq * RET_DK * 2 + 2 * nc * RET_DK * RET_DK * 4) / 2 ** 20 + 8),
        name="retention",
    )(lg, z, z, z, z, gn, s0, ret_prev)
    return ret, sfin


def _conv_kernel(a_ref, b_ref, ap_ref, bp_ref, an_ref, bn_ref, w_ref, cb_ref, lg_ref, lb_ref, o_ref, ext, shifted,
                 *, tm, n_lat_total, n_lat, n_ctx, ktaps):
    i = pl.program_id(0)
    row0 = i * tm
    is_lat = row0 < n_lat_total
    pos = jnp.where(is_lat, row0 % n_lat, (row0 - n_lat_total) % n_ctx)
    seq = jnp.where(is_lat, n_lat, n_ctx)
    keep_prev = (pos != 0).astype(F32)
    keep_next = (pos + tm != seq).astype(F32)

    def glu(a, b):
        return a.astype(F32) * jax.nn.sigmoid(b.astype(F32))

    ext[0:HALO, :] = glu(ap_ref[...], bp_ref[...]) * keep_prev
    ext[HALO:HALO + tm, :] = glu(a_ref[...], b_ref[...])
    ext[HALO + tm:HALO + tm + HALO, :] = glu(an_ref[...], bn_ref[...]) * keep_next

    half = ktaps // 2
    rc = 32
    sub = 8
    span = shifted.shape[1]
    for s in range(1, sub):
        shifted[s - 1] = ext[s:s + span, :]
    for r in range(tm // rc):
        acc = jnp.zeros((rc, a_ref.shape[1]), F32)
        for j in range(ktaps):
            o = HALO + r * rc + j - half
            s, base = o % sub, o - o % sub
            win = ext[base:base + rc, :] if s == 0 else shifted[s - 1, base:base + rc, :]
            acc = acc + w_ref[j:j + 1, :] * win
        y = acc + cb_ref[...]
        mu = jnp.mean(y, axis=-1, keepdims=True)
        dlt = y - mu
        var = jnp.mean(dlt * dlt, axis=-1, keepdims=True)
        yn = dlt * lax.rsqrt(var + EPS) * lg_ref[...] + lb_ref[...]
        o_ref[r * rc:(r + 1) * rc, :] = (yn * jax.nn.sigmoid(yn)).astype(BF16)


def _conv_module(z, conv_w, conv_b, ln_g, ln_b, nb, n_lat, n_ctx):
    nt = z.shape[0]
    ktaps, cw = conv_w.shape
    tm = 256
    assert n_lat % tm == 0 and n_ctx % tm == 0 and ktaps // 2 <= HALO
    a_blk = (2 * RET_HEADS * RET_DK + 2 * RET_HEADS * RET_DK) // cw
    hb = tm // HALO
    last = nt // HALO - 1
    kern = functools.partial(_conv_kernel, tm=tm, n_lat_total=nb * n_lat, n_lat=n_lat, n_ctx=n_ctx, ktaps=ktaps)
    vec = lambda i: (0, 0)
    return pl.pallas_call(
        kern,
        grid=(nt // tm,),
        in_specs=[pl.BlockSpec((tm, cw), lambda i: (i, a_blk)),
                  pl.BlockSpec((tm, cw), lambda i: (i, a_blk + 1)),
                  pl.BlockSpec((HALO, cw), lambda i: (jnp.maximum(i * hb - 1, 0), a_blk)),
                  pl.BlockSpec((HALO, cw), lambda i: (jnp.maximum(i * hb - 1, 0), a_blk + 1)),
                  pl.BlockSpec((HALO, cw), lambda i: (jnp.minimum((i + 1) * hb, last), a_blk)),
                  pl.BlockSpec((HALO, cw), lambda i: (jnp.minimum((i + 1) * hb, last), a_blk + 1)),
                  pl.BlockSpec((ktaps, cw), vec),
                  pl.BlockSpec((1, cw), vec),
                  pl.BlockSpec((1, cw), vec),
                  pl.BlockSpec((1, cw), vec)],
        out_specs=pl.BlockSpec((tm, cw), lambda i: (i, 0)),
        out_shape=jax.ShapeDtypeStruct((nt, cw), BF16),
        scratch_shapes=[pltpu.VMEM((tm + 2 * HALO, cw), F32),
                        pltpu.VMEM((7, tm + 2 * HALO - 8, cw), F32)],
        compiler_params=_cparams(("arbitrary",), 40),
        name="conv_module",
    )(z, z, z, z, z, z, conv_w, conv_b.reshape(1, cw), ln_g.reshape(1, cw), ln_b.reshape(1, cw))


def _outproj_even_kernel(ret_ref, cv_ref, w_ref, x_ref, mod_ref, o_ref):
    rw = ret_ref.shape[1]
    y = jnp.dot(ret_ref[...], w_ref[0:rw, :], preferred_element_type=F32)
    y = y + jnp.dot(cv_ref[...], w_ref[rw:, :], preferred_element_type=F32)
    o_ref[...] = x_ref[...] + mod_ref[0][2:3] * y


def _outproj_even(ret, cv, w_bf, xs, mod_l, nb, n_lat, n_ctx_total):
    nt, d = xs.shape
    tm = _pick_tile(n_lat, n_ctx_total, (512, 256))
    bpb = n_lat // tm
    rw, cw = ret.shape[1], cv.shape[1]
    return pl.pallas_call(
        _outproj_even_kernel,
        grid=(nt // tm,),
        in_specs=[pl.BlockSpec((tm, rw), lambda i: (i, 0)),
                  pl.BlockSpec((tm, cw), lambda i: (i, 0)),
                  pl.BlockSpec((rw + cw, d), lambda i: (0, 0)),
                  pl.BlockSpec((tm, d), lambda i: (i, 0)),
                  pl.BlockSpec((1, 6, d), lambda i: (_group_of(i, bpb, nb), 0, 0))],
        out_specs=pl.BlockSpec((tm, d), lambda i: (i, 0)),
        out_shape=jax.ShapeDtypeStruct((nt, d), F32),
        input_output_aliases={3: 0},
        compiler_params=_cparams(("arbitrary",),
                                 (2 * (rw + cw) * d * 2 + 4 * tm * d * 4 + 2 * tm * (rw + cw) * 2
                                  + tm * d * 4) / 2 ** 20 + 8),
        name="outproj_even",
    )(ret, cv, w_bf, xs, mod_l)


def _premix_odd_kernel(x_ref, mod_ref, g_ref, w_ref, o_ref, h_scr):
    j = pl.program_id(1)

    @pl.when(j == 0)
    def _():
        m = mod_ref[0]
        h_scr[...] = _norm_mod(x_ref[...], g_ref[...], m[0:1], m[1:2]).astype(BF16)

    z = jnp.dot(h_scr[...], w_ref[...], preferred_element_type=F32)
    o_ref[...] = (0.5 * z * (1.0 + lax.erf(z * (2.0 ** -0.5)))).astype(BF16)


def _premix_odd(xs, mod_l, g, w_bf, nb, n_lat, nrows):
    d = xs.shape[1]
    n_out = w_bf.shape[1]
    tm = _pick_tile(n_lat, nrows, (1024, 512, 256))
    tn = 1024
    bpb = n_lat // tm
    return pl.pallas_call(
        _premix_odd_kernel,
        grid=(nrows // tm, n_out // tn),
        in_specs=[pl.BlockSpec((tm, d), lambda i, j: (i, 0)),
                  pl.BlockSpec((1, 6, d), lambda i, j: (_group_of(i, bpb, nb), 0, 0)),
                  pl.BlockSpec((1, d), lambda i, j: (0, 0)),
                  pl.BlockSpec((d, tn), lambda i, j: (0, j))],
        out_specs=pl.BlockSpec((tm, tn), lambda i, j: (i, j)),
        out_shape=jax.ShapeDtypeStruct((nrows, n_out), BF16),
        scratch_shapes=[pltpu.VMEM((tm, d), BF16)],
        compiler_params=_cparams(("arbitrary", "arbitrary"),
                                 (2 * tm * d * 4 + tm * d * 2 + 2 * d * tn * 2 + 2 * tm * tn * 2
                                  + 2 * tm * tn * 4) / 2 ** 20 + 8),
        name="premix_odd",
    )(xs, mod_l, g.reshape(1, d), w_bf)


def _sgu_out_kernel(u_ref, v_ref, lg_ref, lb_ref, ws_ref, bs_ref, w_ref, x_ref, mod_ref, o_ref, p_scr):
    tm, sw = u_ref.shape
    gw = sw // SGU_GROUPS
    v = v_ref[...].astype(F32)
    mu = jnp.mean(v, axis=-1, keepdims=True)
    dlt = v - mu
    var = jnp.mean(dlt * dlt, axis=-1, keepdims=True)
    p_scr[...] = (dlt * lax.rsqrt(var + EPS) * lg_ref[...] + lb_ref[...]).astype(BF16)
    for cc in range(tm // SGU_CHUNK):
        rows = slice(cc * SGU_CHUNK, (cc + 1) * SGU_CHUNK)
        for gi in range(SGU_GROUPS):
            cols = slice(gi * gw, (gi + 1) * gw)
            mixed = jnp.dot(ws_ref[gi], p_scr[rows, cols], preferred_element_type=F32) + bs_ref[:, cols]
            p_scr[rows, cols] = (u_ref[rows, cols].astype(F32) * mixed).astype(BF16)
    y = jnp.dot(p_scr[...], w_ref[...], preferred_element_type=F32)
    o_ref[...] = x_ref[...] + mod_ref[0][2:3] * y


def _sgu_out(uv, ln_g, ln_b, ws_bf, bias_full, w_bf, xs, mod_l, nb, n_lat, nrows):
    nt, d = xs.shape
    sw = w_bf.shape[0]
    tm = _pick_tile(n_lat, nrows, (512, 256))
    bpb = n_lat // tm
    vec = lambda i: (0, 0)
    return pl.pallas_call(
        _sgu_out_kernel,
        grid=(nrows // tm,),
        in_specs=[pl.BlockSpec((tm, sw), lambda i: (i, 0)),
                  pl.BlockSpec((tm, sw), lambda i: (i, 1)),
                  pl.BlockSpec((1, sw), vec),
                  pl.BlockSpec((1, sw), vec),
                  pl.BlockSpec((SGU_GROUPS, SGU_CHUNK, SGU_CHUNK), lambda i: (0, 0, 0)),
                  pl.BlockSpec((SGU_CHUNK, sw), vec),
                  pl.BlockSpec((sw, d), vec),
                  pl.BlockSpec((tm, d), lambda i: (i, 0)),
                  pl.BlockSpec((1, 6, d), lambda i: (_group_of(i, bpb, nb), 0, 0))],
        out_specs=pl.BlockSpec((tm, d), lambda i: (i, 0)),
        out_shape=jax.ShapeDtypeStruct((nt, d), F32),
        scratch_shapes=[pltpu.VMEM((tm, sw), BF16)],
        input_output_aliases={7: 0},
        compiler_params=_cparams(("arbitrary",),
                                 (2 * sw * d * 2 + 4 * tm * d * 4 + 4 * tm * sw * 2 + tm * sw * 2
                                  + 2 * SGU_CHUNK * sw * 4 + 2 * tm * sw * 4) / 2 ** 20 + 8),
        name="sgu_out",
    )(uv, uv, ln_g.reshape(1, sw), ln_b.reshape(1, sw), ws_bf, bias_full, w_bf, xs, mod_l)


def _router_kernel(x_ref, mod_ref, g_ref, whi_ref, wlo_ref, rb_ref, h_ref, aux_ref, cnt_ref, carry):
    i = pl.program_id(0)
    tm = x_ref.shape[0]

    @pl.when(i == 0)
    def _():
        carry[...] = jnp.zeros_like(carry)

    m = mod_ref[0]
    h = _norm_mod(x_ref[...], g_ref[...], m[3:4], m[4:5])
    h_ref[...] = h
    h_hi = h.astype(BF16)
    h_lo = (h - h_hi.astype(F32)).astype(BF16)
    logits = (jnp.dot(h_hi, whi_ref[...], preferred_element_type=F32)
              + jnp.dot(h_lo, whi_ref[...], preferred_element_type=F32)
              + jnp.dot(h_hi, wlo_ref[...], preferred_element_type=F32)) + rb_ref[...]

    lane = lax.broadcasted_iota(jnp.int32, (tm, LANES), 1).astype(F32)
    vals, idxs = [], []
    l = logits
    for _ in range(TOP_K):
        mx = jnp.max(l, axis=-1, keepdims=True)
        ix = jnp.min(jnp.where(l == mx, lane, float(LANES)), axis=-1, keepdims=True)
        vals.append(mx)
        idxs.append(ix)
        l = jnp.where(lane == ix, -jnp.inf, l)
    es = [jnp.exp(v - vals[0]) for v in vals]
    den = es[0]
    for e in es[1:]:
        den = den + e

    sel = jnp.zeros((tm, LANES), F32)
    for ix in idxs:
        sel = sel + jnp.where(lane == ix, 1.0, 0.0)
    rr = lax.broadcasted_iota(jnp.int32, (tm, tm), 0)
    rc = lax.broadcasted_iota(jnp.int32, (tm, tm), 1)
    ltri = jnp.where(rc < rr, 1.0, 0.0).astype(BF16)
    rank = jnp.dot(ltri, sel.astype(BF16), preferred_element_type=F32) + carry[...]
    carry[...] = carry[...] + jnp.sum(sel, axis=0, keepdims=True)
    cnt_ref[...] = carry[...]

    aux = jnp.zeros((tm, LANES), F32)
    for r in range(TOP_K):
        rk = jnp.sum(jnp.where(lane == idxs[r], rank, 0.0), axis=-1, keepdims=True)
        aux = jnp.where(lane == float(r), idxs[r], aux)
        aux = jnp.where(lane == float(TOP_K + r), rk, aux)
        aux = jnp.where(lane == float(2 * TOP_K + r), es[r] / den, aux)
    aux_ref[...] = aux


def _router(xs, mod_l, g, whi, wlo, rb, nb, n_lat, nrows):
    d = xs.shape[1]
    tm = 256
    bpb = n_lat // tm
    vec = lambda i: (0, 0)
    return pl.pallas_call(
        _router_kernel,
        grid=(nrows // tm,),
        in_specs=[pl.BlockSpec((tm, d), lambda i: (i, 0)),
                  pl.BlockSpec((1, 6, d), lambda i: (_group_of(i, bpb, nb), 0, 0)),
                  pl.BlockSpec((1, d), vec),
                  pl.BlockSpec((d, LANES), vec),
                  pl.BlockSpec((d, LANES), vec),
                  pl.BlockSpec((1, LANES), vec)],
        out_specs=[pl.BlockSpec((tm, d), lambda i: (i, 0)),
                   pl.BlockSpec((tm, LANES), lambda i: (i, 0)),
                   pl.BlockSpec((1, LANES), vec)],
        out_shape=[jax.ShapeDtypeStruct((nrows, d), F32),
                   jax.ShapeDtypeStruct((nrows, LANES), F32),
                   jax.ShapeDtypeStruct((1, LANES), F32)],
        scratch_shapes=[pltpu.VMEM((1, LANES), F32)],
        compiler_params=_cparams(("arbitrary",), 32),
        name="router",
    )(xs, mod_l, g.reshape(1, d), whi, wlo, rb)


def _dispatch_kernel(pos_ref, h_ref, hs_ref, sem):
    tm = h_ref.shape[0]

    def row_copy(t, p):
        return pltpu.make_async_copy(h_ref.at[pl.ds(t, 1), :], hs_ref.at[pl.ds(p, 1), :], sem)

    def issue(t, carry):
        for r in range(TOP_K):
            row_copy(t, pos_ref[0, 0, t * TOP_K + r]).start()
        return carry

    lax.fori_loop(0, tm, issue, 0)
    for _ in range(TOP_K):
        pltpu.make_async_copy(h_ref, hs_ref.at[pl.ds(0, tm), :], sem).wait()


def _dispatch(h, pos):
    nrows, d = h.shape
    tm = 256
    pos3 = pos.reshape(nrows // tm, 1, tm * TOP_K)
    return pl.pallas_call(
        _dispatch_kernel,
        grid=(nrows // tm,),
        in_specs=[pl.BlockSpec((1, 1, tm * TOP_K), lambda i: (i, 0, 0), memory_space=pltpu.SMEM),
                  pl.BlockSpec((tm, d), lambda i: (i, 0))],
        out_specs=pl.BlockSpec(memory_space=pl.ANY),
        out_shape=jax.ShapeDtypeStruct((nrows * TOP_K, d), F32),
        scratch_shapes=[pltpu.SemaphoreType.DMA(())],
        compiler_params=_cparams(("arbitrary",), 16),
        name="moe_dispatch",
    )(pos3, h)


def _pair_perm():
    i = jnp.arange(2 * LANES)[:, None]
    j = jnp.arange(2 * LANES)[None, :]
    return jnp.where(j < LANES, i == 2 * j, i == 2 * (j - LANES) + 1).astype(BF16)


def _prep_wgu_kernel(w_ref, p_ref, o_ref):
    blk = 2 * LANES
    for b in range(w_ref.shape[2] // blk):
        cols = slice(b * blk, (b + 1) * blk)
        o_ref[0, :, cols] = jnp.dot(w_ref[0, :, cols].astype(BF16), p_ref[...],
                                    preferred_element_type=F32).astype(BF16)


def _prep_wgu(wgu):
    lead = wgu.shape[:-2]
    d, n2 = wgu.shape[-2:]
    ne = math.prod(lead)
    tr = 1024 if d % 1024 == 0 else d
    out = pl.pallas_call(
        _prep_wgu_kernel,
        grid=(ne, d // tr),
        in_specs=[pl.BlockSpec((1, tr, n2), lambda e, r: (e, r, 0)),
                  pl.BlockSpec((2 * LANES, 2 * LANES), lambda e, r: (0, 0))],
        out_specs=pl.BlockSpec((1, tr, n2), lambda e, r: (e, r, 0)),
        out_shape=jax.ShapeDtypeStruct((ne, d, n2), BF16),
        compiler_params=_cparams(("arbitrary", "arbitrary"), 3 * tr * n2 * 4 / 2 ** 20 + 8),
        name="prep_wgu",
    )(wgu.reshape(ne, d, n2), _pair_perm())
    return out.reshape(lead + (d, n2))


def _expert_kernel(tile_ref, exp_ref, lo_ref, hi_ref, first_ref, hs_ref, wgu_ref, bgu_ref, wd_ref, bd_ref, o_ref):
    del tile_ref, exp_ref
    w = pl.program_id(0)
    tm = hs_ref.shape[0]
    z = jnp.dot(hs_ref[...].astype(BF16), wgu_ref[0], preferred_element_type=F32) + bgu_ref[0]
    parts = []
    for b in range(z.shape[1] // (2 * LANES)):
        zg = jnp.minimum(z[:, 2 * b * LANES:(2 * b + 1) * LANES], SWIGLU_LIMIT)
        zu = jnp.clip(z[:, (2 * b + 1) * LANES:(2 * b + 2) * LANES], -SWIGLU_LIMIT, SWIGLU_LIMIT)
        parts.append(((zu + 1.0) * (zg * jax.nn.sigmoid(SWIGLU_ALPHA * zg))).astype(BF16))
    act = jnp.concatenate(parts, axis=1)
    y = jnp.dot(act, wd_ref[0], preferred_element_type=F32) + bd_ref[0]
    row = lax.broadcasted_iota(jnp.int32, (tm, 1), 0)
    mine = (row >= lo_ref[w]) & (row < hi_ref[w])

    @pl.when(first_ref[w] == 1)
    def _():
        o_ref[...] = jnp.where(mine, y, 0.0)

    @pl.when(first_ref[w] == 0)
    def _():
        o_ref[...] = jnp.where(mine, y, o_ref[...])


def _experts(hs, items, wgu_bf, bgu, wd_bf, bd, tm, ebase):
    p, d = hs.shape
    n2 = wgu_bf.shape[2]
    ne = bgu.shape[0]
    de = n2 // 2
    nw = items[0].shape[0]
    grid_spec = pltpu.PrefetchScalarGridSpec(
        num_scalar_prefetch=5,
        grid=(nw,),
        in_specs=[pl.BlockSpec((tm, d), lambda w, t, e, lo, hi, f: (t[w], 0)),
                  pl.BlockSpec((1, d, n2), lambda w, t, e, lo, hi, f: (ebase + e[w], 0, 0)),
                  pl.BlockSpec((1, 1, n2), lambda w, t, e, lo, hi, f: (e[w], 0, 0)),
                  pl.BlockSpec((1, de, d), lambda w, t, e, lo, hi, f: (ebase + e[w], 0, 0)),
                  pl.BlockSpec((1, 1, d), lambda w, t, e, lo, hi, f: (e[w], 0, 0))],
        out_specs=pl.BlockSpec((tm, d), lambda w, t, e, lo, hi, f: (t[w], 0)),
    )
    return pl.pallas_call(
        _expert_kernel,
        grid_spec=grid_spec,
        out_shape=jax.ShapeDtypeStruct((p, d), F32),
        compiler_params=_cparams(("arbitrary",),
                                 (4 * tm * d * 4 + 2 * d * n2 * 2 + 2 * de * d * 2 + 2 * tm * n2 * 4
                                  + tm * d * 4) / 2 ** 20 + 6),
        name="moe_experts",
    )(*items, hs, wgu_bf, bgu.reshape(ne, 1, n2), wd_bf, bd.reshape(ne, 1, d))


def _expert_items(counts, p, tm, ne):
    ntiles = p // tm
    nw = ntiles + ne - 1
    off = jnp.concatenate([jnp.zeros((1,), jnp.int32), jnp.cumsum(counts)])
    first_tile = off[:-1] // tm
    last_tile = (off[1:] - 1) // tm
    ntile_e = jnp.where(counts > 0, last_tile - first_tile + 1, 0)
    base_incl = jnp.cumsum(ntile_e)
    base = base_incl - ntile_e
    total = base_incl[-1]
    w = jnp.arange(nw, dtype=jnp.int32)
    wc = jnp.minimum(w, total - 1)
    e = jnp.sum((base_incl[None, :] <= wc[:, None]).astype(jnp.int32), axis=1)
    t = first_tile[e] + (wc - base[e])
    lo = jnp.clip(off[e] - t * tm, 0, tm)
    hi = jnp.clip(off[e + 1] - t * tm, 0, tm)
    valid = w < total
    lo = jnp.where(valid, lo, 0)
    hi = jnp.where(valid, hi, 0)
    prev_t = jnp.concatenate([jnp.full((1,), -1, jnp.int32), t[:-1]])
    first = (valid & (t != prev_t)).astype(jnp.int32)
    return (t.astype(jnp.int32), e, lo.astype(jnp.int32), hi.astype(jnp.int32), first), off


def _combine_kernel(pos_ref, ys_ref, aux_ref, x_ref, mod_ref, fg_ref, o_ref, buf, sem, *, final):
    tm = x_ref.shape[0]

    def row_copy(t, r, p):
        return pltpu.make_async_copy(ys_ref.at[pl.ds(p, 1), :], buf.at[r, pl.ds(t, 1), :], sem)

    def issue(t, carry):
        for r in range(TOP_K):
            row_copy(t, r, pos_ref[0, 0, t * TOP_K + r]).start()
        return carry

    lax.fori_loop(0, tm, issue, 0)
    for r in range(TOP_K):
        pltpu.make_async_copy(ys_ref.at[pl.ds(0, tm), :], buf.at[r], sem).wait()

    aux = aux_ref[...]
    y = buf[0] * aux[:, 2 * TOP_K:2 * TOP_K + 1]
    for r in range(1, TOP_K):
        y = y + buf[r] * aux[:, 2 * TOP_K + r:2 * TOP_K + r + 1]
    xn = x_ref[...] + mod_ref[0][5:6] * y
    if final:
        ms = jnp.mean(xn * xn, axis=-1, keepdims=True)
        xn = xn * lax.rsqrt(ms + EPS) * fg_ref[...]
    o_ref[...] = xn


def _combine(ys, pos, aux, xs, mod_l, final_g, nb, n_lat, nrows, final):
    d = xs.shape[1]
    tm = 256
    bpb = n_lat // tm
    pos3 = pos.reshape(nrows // tm, 1, tm * TOP_K)
    out_rows = nrows if final else xs.shape[0]
    return pl.pallas_call(
        functools.partial(_combine_kernel, final=final),
        grid=(nrows // tm,),
        in_specs=[pl.BlockSpec((1, 1, tm * TOP_K), lambda i: (i, 0, 0), memory_space=pltpu.SMEM),
                  pl.BlockSpec(memory_space=pl.ANY),
                  pl.BlockSpec((tm, LANES), lambda i: (i, 0)),
                  pl.BlockSpec((tm, d), lambda i: (i, 0)),
                  pl.BlockSpec((1, 6, d), lambda i: (_group_of(i, bpb, nb), 0, 0)),
                  pl.BlockSpec((1, d), lambda i: (0, 0))],
        out_specs=pl.BlockSpec((tm, d), lambda i: (i, 0)),
        out_shape=jax.ShapeDtypeStruct((out_rows, d), F32),
        scratch_shapes=[pltpu.VMEM((TOP_K, tm, d), F32), pltpu.SemaphoreType.DMA(())],
        input_output_aliases={} if final else {3: 0},
        compiler_params=_cparams(("arbitrary",), (TOP_K * tm * d * 4 + 6 * tm * d * 4) / 2 ** 20 + 8),
        name="moe_combine",
    )(pos3, ys, aux, xs, mod_l, final_g.reshape(1, d))


def _moe(xs, mod_l, g, rw, rb, wgu_bf, bgu, wd_bf, bd, ebase, final_g, nb, n_lat, nrows, final):
    d = xs.shape[1]
    ne = rw.shape[1]
    rw_pad = jnp.zeros((d, LANES), F32).at[:, :ne].set(rw.astype(F32))
    whi = rw_pad.astype(BF16)
    wlo = (rw_pad - whi.astype(F32)).astype(BF16)
    rb_pad = jnp.full((1, LANES), -1e30, F32).at[0, :ne].set(rb.astype(F32))
    h, aux, cnt = _router(xs, mod_l, g, whi, wlo, rb_pad, nb, n_lat, nrows)

    counts = cnt[0, :ne].astype(jnp.int32)
    tm_e = 512
    p = nrows * TOP_K
    items, off = _expert_items(counts, p, tm_e, ne)
    idx = aux[:, 0:TOP_K].astype(jnp.int32)
    rank = aux[:, TOP_K:2 * TOP_K].astype(jnp.int32)
    pos = (off[idx] + rank).reshape(-1)

    hs = _dispatch(h, pos)
    n2 = bgu.shape[1]
    bgu_p = bgu.astype(F32).reshape(ne, n2 // (2 * LANES), LANES, 2).transpose(0, 1, 3, 2).reshape(ne, n2)
    ys = _experts(hs, items, wgu_bf, bgu_p, wd_bf, bd.astype(F32), tm_e, ebase)
    return _combine(ys, pos, aux, xs, mod_l, final_g, nb, n_lat, nrows, final)


def _rope_tables(nb, n_lat, n_ctx):
    def angles(seq_pos, row_pos, col_pos):
        parts = []
        for posv, dim in zip((seq_pos, row_pos, col_pos), ROPE_AXES):
            inv = ROPE_THETA ** (-jnp.arange(0, dim, 2, dtype=F32) / dim)
            parts.append(posv.astype(F32)[:, None] * inv[None, :])
        return jnp.concatenate(parts, axis=-1)

    rows = n_lat // GRID_W
    lat = angles(jnp.full((n_lat,), n_ctx, jnp.int32),
                 jnp.repeat(jnp.arange(rows, dtype=jnp.int32), GRID_W),
                 jnp.tile(jnp.arange(GRID_W, dtype=jnp.int32), rows))
    zl = jnp.zeros((n_ctx,), jnp.int32)
    ctx = angles(jnp.arange(n_ctx, dtype=jnp.int32), zl, zl)
    ang = jnp.concatenate([lat, jnp.tile(ctx, (nb, 1))], axis=0)
    cos, sin = jnp.cos(ang), jnp.sin(ang)
    return jnp.concatenate([cos, cos], axis=-1), jnp.concatenate([-sin, sin], axis=-1)


def _deinterleave_heads(w):
    d, n = w.shape
    w4 = w.reshape(d, n // RET_DK, RET_DK // 2, 2)
    return jnp.concatenate([w4[..., 0], w4[..., 1]], axis=-1).reshape(d, n)


def kernel(x, c, ctx, c_ctx, mod_w, mod_b, norm_mix_g, norm_ffn_g, ev_w_in, ev_w_out, ret_decay_f, ret_decay_b,
           ret_gn_g, conv_w, conv_b, conv_ln_g, conv_ln_b, od_w_in, od_w_out, sgu_ln_g, sgu_ln_b, sgu_w, sgu_b,
           router_w, router_b, moe_w_gu, moe_b_gu, moe_w_down, moe_b_down, final_g):
    nb, n_lat, d = x.shape
    n_ctx = ctx.shape[1]
    depth = mod_w.shape[0]
    n_lat_total = nb * n_lat
    n_ctx_total = nb * n_ctx
    nt = n_lat_total + n_ctx_total
    assert nb < MOD_GROUPS and n_lat % GRID_W == 0
    qk = RET_HEADS * RET_DK

    xs = jnp.concatenate([x.reshape(n_lat_total, d), ctx.reshape(n_ctx_total, d)], axis=0).astype(F32)
    cc = jnp.zeros((MOD_GROUPS, d), F32).at[:nb].set(c.astype(F32)).at[nb].set(c_ctx.astype(F32))
    mod = _mod_table(cc, mod_w, mod_b).reshape(depth, MOD_GROUPS, 6, d)
    cos_t, sin_t = _rope_tables(nb, n_lat, n_ctx)
    ne = router_w.shape[2]
    wgu_bf = _prep_wgu(moe_w_gu.reshape((depth * ne,) + moe_w_gu.shape[2:]))
    wd_bf = moe_w_down.reshape((depth * ne,) + moe_w_down.shape[2:]).astype(BF16)

    out = None
    for layer in range(depth):
        last = layer == depth - 1
        nrows = n_lat_total if last else nt
        mod_l = mod[layer]
        if layer % 2 == 0:
            e = layer // 2
            w_in = ev_w_in[e]
            w_in = jnp.concatenate([_deinterleave_heads(w_in[:, :qk]), _deinterleave_heads(w_in[:, qk:2 * qk]),
                                    w_in[:, 2 * qk:]], axis=1).astype(BF16)
            z = _premix_even(xs, mod_l, norm_mix_g[layer], w_in, cos_t, sin_t, nb, n_lat, n_ctx_total)
            lg = jnp.stack([-jnp.exp(ret_decay_f[e].astype(F32)), -jnp.exp(ret_decay_b[e].astype(F32))])
            gn = ret_gn_g[e].reshape(1, qk).astype(F32)
            s0 = jnp.zeros((nb, RET_HEADS, 2, RET_DK, RET_DK), F32)
            ret0 = jnp.zeros((nt, qk), BF16)
            ret, st = _retention(z, lg, gn, s0, ret0, nb, n_ctx, n_lat_total)
            ret, _ = _retention(z, lg, gn, st, ret, nb, n_lat, 0)
            cv = _conv_module(z, conv_w[e].astype(F32), conv_b[e].astype(F32), conv_ln_g[e].astype(F32),
                              conv_ln_b[e].astype(F32), nb, n_lat, n_ctx)
            xs = _outproj_even(ret, cv, ev_w_out[e].astype(BF16), xs, mod_l, nb, n_lat, n_ctx_total)
        else:
            j = layer // 2
            uv = _premix_odd(xs, mod_l, norm_mix_g[layer], od_w_in[j].astype(BF16), nb, n_lat, nrows)
            sw = od_w_out.shape[1]
            bias_full = jnp.repeat(sgu_b[j].astype(F32).T, sw // SGU_GROUPS, axis=1)
            xs = _sgu_out(uv, sgu_ln_g[j].astype(F32), sgu_ln_b[j].astype(F32), sgu_w[j].astype(BF16), bias_full,
                          od_w_out[j].astype(BF16), xs, mod_l, nb, n_lat, nrows)
        res = _moe(xs, mod_l, norm_ffn_g[layer], router_w[layer], router_b[layer], wgu_bf, moe_b_gu[layer],
                   wd_bf, moe_b_down[layer], layer * ne, final_g.astype(F32), nb, n_lat, nrows, last)
        if last:
            out = res
        else:
            xs = res
    return out.reshape(nb, n_lat, d).astype(x.dtype)
```

```python
import functools
import math

import jax
import jax.numpy as jnp
from jax import lax
from jax.experimental import pallas as pl
from jax.experimental.pallas import tpu as pltpu

F32 = jnp.float32
BF16 = jnp.bfloat16

LANES = 128
CHUNK_ROWS = 8
RET_HEADS = 8
RET_DK = 128
RET_CHUNK = 128
ROPE_AXES = (32, 48, 48)
ROPE_THETA = 10000.0
GRID_W = 64
SGU_GROUPS = 8
SGU_CHUNK = 128
TOP_K = 4
SWIGLU_LIMIT = 7.0
SWIGLU_ALPHA = 1.702
EPS = 1e-6
MOD_GROUPS = 8
HALO = 16
VMEM_CAP = 58 * 2 ** 20


def _cparams(sem, vmem_mb):
    return pltpu.CompilerParams(dimension_semantics=sem,
                                vmem_limit_bytes=min(int(vmem_mb * 2 ** 20), VMEM_CAP))


def _pick_tile(n_lat, n_ctx, cands):
    for t in cands:
        if n_lat % t == 0 and n_ctx % t == 0:
            return t
    raise ValueError("no token tile divides the latent and context lengths")


def _group_of(i, blocks_per_batch, nb):
    return jnp.minimum(i // blocks_per_batch, nb)


def _norm_mod(x, g, shift, scale):
    ms = jnp.mean(x * x, axis=-1, keepdims=True)
    return (x * lax.rsqrt(ms + EPS) * g) * (1.0 + scale) + shift


def _mod_kernel(c_ref, w_ref, b_ref, o_ref):
    c = c_ref[...]
    s = (c * jax.nn.sigmoid(c)).astype(BF16)
    o_ref[0] = jnp.dot(s, w_ref[0].astype(BF16), preferred_element_type=F32) + b_ref[0]


def _mod_table(cc, mod_w, mod_b):
    depth, d, n6 = mod_w.shape
    tn = min(d, 1024)
    return pl.pallas_call(
        _mod_kernel,
        grid=(depth, n6 // tn),
        in_specs=[pl.BlockSpec((MOD_GROUPS, d), lambda l, j: (0, 0)),
                  pl.BlockSpec((1, d, tn), lambda l, j: (l, 0, j)),
                  pl.BlockSpec((1, 1, tn), lambda l, j: (l, 0, j))],
        out_specs=pl.BlockSpec((1, MOD_GROUPS, tn), lambda l, j: (l, 0, j)),
        out_shape=jax.ShapeDtypeStruct((depth, MOD_GROUPS, n6), F32),
        compiler_params=_cparams(("arbitrary", "arbitrary"), 2 * d * tn * 4 / 2 ** 20 + 8),
        name="mod_table",
    )(cc, mod_w, mod_b.reshape(depth, 1, n6))


def _premix_even_kernel(x_ref, mod_ref, g_ref, w_ref, cos_ref, sin_ref, o_ref, h_scr, *, kscale):
    j = pl.program_id(1)

    @pl.when(j == 0)
    def _():
        m = mod_ref[0]
        h_scr[...] = _norm_mod(x_ref[...], g_ref[...], m[0:1], m[1:2]).astype(BF16)

    z = jnp.dot(h_scr[...], w_ref[...], preferred_element_type=F32)

    @pl.when(j < 2)
    def _():
        c = cos_ref[...]
        s = sin_ref[...]
        scale = jnp.where(j == 1, kscale, 1.0).astype(F32)
        for hh in range(z.shape[1] // RET_DK):
            t = z[:, hh * RET_DK:(hh + 1) * RET_DK]
            r = pltpu.roll(t, RET_DK // 2, axis=1)
            o_ref[:, hh * RET_DK:(hh + 1) * RET_DK] = ((t * c + r * s) * scale).astype(BF16)

    @pl.when(j >= 2)
    def _():
        o_ref[...] = z.astype(BF16)


def _premix_even(xs, mod_l, g, w_bf, cos_t, sin_t, nb, n_lat, n_ctx_total):
    nt, d = xs.shape
    n_out = w_bf.shape[1]
    tm = _pick_tile(n_lat, n_ctx_total, (1024, 512, 256))
    tn = RET_HEADS * RET_DK
    bpb = n_lat // tm
    n_lat_blocks = nb * bpb

    def tab_idx(i, j):
        return (jnp.where(i < n_lat_blocks, i % bpb, bpb + (i - n_lat_blocks)), 0)

    return pl.pallas_call(
        functools.partial(_premix_even_kernel, kscale=RET_DK ** -0.5),
        grid=(nt // tm, n_out // tn),
        in_specs=[pl.BlockSpec((tm, d), lambda i, j: (i, 0)),
                  pl.BlockSpec((1, 6, d), lambda i, j: (_group_of(i, bpb, nb), 0, 0)),
                  pl.BlockSpec((1, d), lambda i, j: (0, 0)),
                  pl.BlockSpec((d, tn), lambda i, j: (0, j)),
                  pl.BlockSpec((tm, RET_DK), tab_idx),
                  pl.BlockSpec((tm, RET_DK), tab_idx)],
        out_specs=pl.BlockSpec((tm, tn), lambda i, j: (i, j)),
        out_shape=jax.ShapeDtypeStruct((nt, n_out), BF16),
        scratch_shapes=[pltpu.VMEM((tm, d), BF16)],
        compiler_params=_cparams(("arbitrary", "arbitrary"),
                                 (2 * tm * d * 4 + tm * d * 2 + 2 * d * tn * 2 + 2 * tm * tn * 2
                                  + 2 * tm * tn * 4) / 2 ** 20 + 8),
        name="premix_even",
    )(xs, mod_l, g.reshape(1, d), w_bf, cos_t, sin_t)


def _retention_kernel(lg_ref, q_ref, k_ref, v_ref, g_ref, gn_ref, s0_ref, ret_in_ref, ret_ref, sfin_ref,
                      st_scr, *, nc, unroll):
    del ret_in_ref
    hh = pl.program_id(1)
    lgf = lg_ref[0, hh]
    lgb = lg_ref[1, hh]
    c = RET_CHUNK
    ri = lax.broadcasted_iota(jnp.int32, (c, c), 0).astype(F32)
    ci = lax.broadcasted_iota(jnp.int32, (c, c), 1).astype(F32)
    rel = ri - ci
    dmat = jnp.where(rel >= 0, jnp.exp(lgf * jnp.maximum(rel, 0.0)), jnp.exp(lgb * jnp.maximum(-rel, 0.0)))
    xi_f = jnp.exp(lgf * (ri + 1.0))
    xi_b = jnp.exp(lgb * (c - ri))
    zeta_f = jnp.exp(lgf * (c - 1.0 - ci))
    zeta_b = jnp.exp(lgb * ci)
    gc_f = jnp.exp(lgf * c + jnp.zeros((c, c), F32))
    gc_b = jnp.exp(lgb * c + jnp.zeros((c, c), F32))

    def local_sums(i, carry):
        for u in range(unroll):
            cc = i * unroll + u
            r0 = pl.multiple_of(cc * c, c)
            kt = k_ref[pl.ds(r0, c), :].astype(F32).T
            v = v_ref[pl.ds(r0, c), :]
            st_scr[cc, :, 0:c] = jnp.dot((kt * zeta_f).astype(BF16), v, preferred_element_type=F32)
            st_scr[cc, :, c:2 * c] = jnp.dot((kt * zeta_b).astype(BF16), v, preferred_element_type=F32)
        return carry

    lax.fori_loop(0, nc // unroll, local_sums, 0)

    def scans(i, carry):
        sf, sb = carry
        cb = nc - 1 - i
        uf = st_scr[i, :, 0:c]
        st_scr[i, :, 0:c] = sf
        ub = st_scr[cb, :, c:2 * c]
        st_scr[cb, :, c:2 * c] = sb
        return gc_f * sf + uf, gc_b * sb + ub

    sf, sb = lax.fori_loop(0, nc, scans, (s0_ref[0, 0, 0], s0_ref[0, 0, 1]))
    sfin_ref[0, 0, 0] = sf
    sfin_ref[0, 0, 1] = sb

    def outputs(i, carry):
        for u in range(unroll):
            cc = i * unroll + u
            r0 = pl.multiple_of(cc * c, c)
            q = q_ref[pl.ds(r0, c), :]
            a = lax.dot_general(q, k_ref[pl.ds(r0, c), :], (((1,), (1,)), ((), ())), preferred_element_type=F32)
            o = jnp.dot((a * dmat).astype(BF16), v_ref[pl.ds(r0, c), :], preferred_element_type=F32)
            cr = jnp.dot(q, st_scr[cc].astype(BF16), preferred_element_type=F32)
            o = o + cr[:, 0:c] * xi_f + cr[:, c:2 * c] * xi_b
            mu = jnp.mean(o, axis=-1, keepdims=True)
            dlt = o - mu
            var = jnp.mean(dlt * dlt, axis=-1, keepdims=True)
            y = dlt * lax.rsqrt(var + EPS) * gn_ref[...]
            g = g_ref[pl.ds(r0, c), :].astype(F32)
            ret_ref[pl.ds(r0, c), :] = (g * jax.nn.sigmoid(g) * y).astype(BF16)
        return carry

    lax.fori_loop(0, nc // unroll, outputs, 0)


def _retention(z, lg, gn, s0, ret_prev, nb, seq, row0):
    nt = z.shape[0]
    nc = seq // RET_CHUNK
    rb0 = row0 // seq
    hcols = RET_HEADS

    def col(off):
        return lambda b, h: (rb0 + b, off * hcols + h)

    ret, sfin = pl.pallas_call(
        functools.partial(_retention_kernel, nc=nc, unroll=max(u for u in (8, 4, 2, 1) if nc % u == 0)),
        grid=(nb, RET_HEADS),
        in_specs=[pl.BlockSpec(memory_space=pltpu.SMEM),
                  pl.BlockSpec((seq, RET_DK), col(0)),
                  pl.BlockSpec((seq, RET_DK), col(1)),
                  pl.BlockSpec((seq, RET_DK), col(2)),
                  pl.BlockSpec((seq, RET_DK), col(3)),
                  pl.BlockSpec((1, RET_DK), lambda b, h: (0, h)),
                  pl.BlockSpec((1, 1, 2, RET_DK, RET_DK), lambda b, h: (b, h, 0, 0, 0)),
                  pl.BlockSpec(memory_space=pl.ANY)],
        out_specs=[pl.BlockSpec((seq, RET_DK), lambda b, h: (rb0 + b, h)),
                   pl.BlockSpec((1, 1, 2, RET_DK, RET_DK), lambda b, h: (b, h, 0, 0, 0))],
        out_shape=[jax.ShapeDtypeStruct((nt, RET_HEADS * RET_DK), BF16),
                   jax.ShapeDtypeStruct((nb, RET_HEADS, 2, RET_DK, RET_DK), F32)],
        scratch_shapes=[pltpu.VMEM((nc, RET_DK, 2 * RET_DK), F32)],
        input_output_aliases={7: 0},
        compiler_params=_cparams(("arbitrary", "arbitrary"),
                                 (10 * seq * RET_DK * 2 + 2 * nc * RET_DK * RET_DK * 4) / 2 ** 20 + 8),
        name="retention",
    )(lg, z, z, z, z, gn, s0, ret_prev)
    return ret, sfin


def _conv_kernel(a_ref, b_ref, ap_ref, bp_ref, an_ref, bn_ref, w_ref, cb_ref, lg_ref, lb_ref, o_ref, ext, shifted,
                 *, tm, n_lat_total, n_lat, n_ctx, ktaps):
    i = pl.program_id(0)
    row0 = i * tm
    is_lat = row0 < n_lat_total
    pos = jnp.where(is_lat, row0 % n_lat, (row0 - n_lat_total) % n_ctx)
    seq = jnp.where(is_lat, n_lat, n_ctx)
    keep_prev = (pos != 0).astype(F32)
    keep_next = (pos + tm != seq).astype(F32)

    def glu(a, b):
        return a.astype(F32) * jax.nn.sigmoid(b.astype(F32))

    ext[0:HALO, :] = glu(ap_ref[...], bp_ref[...]) * keep_prev
    ext[HALO:HALO + tm, :] = glu(a_ref[...], b_ref[...])
    ext[HALO + tm:HALO + tm + HALO, :] = glu(an_ref[...], bn_ref[...]) * keep_next

    half = ktaps // 2
    rc = 32
    sub = 8
    span = shifted.shape[1]
    for s in range(1, sub):
        shifted[s - 1] = ext[s:s + span, :]
    for r in range(tm // rc):
        acc = jnp.zeros((rc, a_ref.shape[1]), F32)
        for j in range(ktaps):
            o = HALO + r * rc + j - half
            s, base = o % sub, o - o % sub
            win = ext[base:base + rc, :] if s == 0 else shifted[s - 1, base:base + rc, :]
            acc = acc + w_ref[j:j + 1, :] * win
        y = acc + cb_ref[...]
        mu = jnp.mean(y, axis=-1, keepdims=True)
        dlt = y - mu
        var = jnp.mean(dlt * dlt, axis=-1, keepdims=True)
        yn = dlt * lax.rsqrt(var + EPS) * lg_ref[...] + lb_ref[...]
        o_ref[r * rc:(r + 1) * rc, :] = (yn * jax.nn.sigmoid(yn)).astype(BF16)


def _conv_module(z, conv_w, conv_b, ln_g, ln_b, nb, n_lat, n_ctx):
    nt = z.shape[0]
    ktaps, cw = conv_w.shape
    tm = 256
    assert n_lat % tm == 0 and n_ctx % tm == 0 and ktaps // 2 <= HALO
    a_blk = (2 * RET_HEADS * RET_DK + 2 * RET_HEADS * RET_DK) // cw
    hb = tm // HALO
    last = nt // HALO - 1
    kern = functools.partial(_conv_kernel, tm=tm, n_lat_total=nb * n_lat, n_lat=n_lat, n_ctx=n_ctx, ktaps=ktaps)
    vec = lambda i: (0, 0)
    return pl.pallas_call(
        kern,
        grid=(nt // tm,),
        in_specs=[pl.BlockSpec((tm, cw), lambda i: (i, a_blk)),
                  pl.BlockSpec((tm, cw), lambda i: (i, a_blk + 1)),
                  pl.BlockSpec((HALO, cw), lambda i: (jnp.maximum(i * hb - 1, 0), a_blk)),
                  pl.BlockSpec((HALO, cw), lambda i: (jnp.maximum(i * hb - 1, 0), a_blk + 1)),
                  pl.BlockSpec((HALO, cw), lambda i: (jnp.minimum((i + 1) * hb, last), a_blk)),
                  pl.BlockSpec((HALO, cw), lambda i: (jnp.minimum((i + 1) * hb, last), a_blk + 1)),
                  pl.BlockSpec((ktaps, cw), vec),
                  pl.BlockSpec((1, cw), vec),
                  pl.BlockSpec((1, cw), vec),
                  pl.BlockSpec((1, cw), vec)],
        out_specs=pl.BlockSpec((tm, cw), lambda i: (i, 0)),
        out_shape=jax.ShapeDtypeStruct((nt, cw), BF16),
        scratch_shapes=[pltpu.VMEM((tm + 2 * HALO, cw), F32),
                        pltpu.VMEM((7, tm + 2 * HALO - 8, cw), F32)],
        compiler_params=_cparams(("arbitrary",), 40),
        name="conv_module",
    )(z, z, z, z, z, z, conv_w, conv_b.reshape(1, cw), ln_g.reshape(1, cw), ln_b.reshape(1, cw))


def _outproj_even_kernel(ret_ref, cv_ref, w_ref, x_ref, mod_ref, o_ref):
    rw = ret_ref.shape[1]
    y = jnp.dot(ret_ref[...], w_ref[0:rw, :], preferred_element_type=F32)
    y = y + jnp.dot(cv_ref[...], w_ref[rw:, :], preferred_element_type=F32)
    o_ref[...] = x_ref[...] + mod_ref[0][2:3] * y


def _outproj_even(ret, cv, w_bf, xs, mod_l, nb, n_lat, n_ctx_total):
    nt, d = xs.shape
    tm = _pick_tile(n_lat, n_ctx_total, (512, 256))
    bpb = n_lat // tm
    rw, cw = ret.shape[1], cv.shape[1]
    return pl.pallas_call(
        _outproj_even_kernel,
        grid=(nt // tm,),
        in_specs=[pl.BlockSpec((tm, rw), lambda i: (i, 0)),
                  pl.BlockSpec((tm, cw), lambda i: (i, 0)),
                  pl.BlockSpec((rw + cw, d), lambda i: (0, 0)),
                  pl.BlockSpec((tm, d), lambda i: (i, 0)),
                  pl.BlockSpec((1, 6, d), lambda i: (_group_of(i, bpb, nb), 0, 0))],
        out_specs=pl.BlockSpec((tm, d), lambda i: (i, 0)),
        out_shape=jax.ShapeDtypeStruct((nt, d), F32),
        input_output_aliases={3: 0},
        compiler_params=_cparams(("arbitrary",),
                                 (2 * (rw + cw) * d * 2 + 4 * tm * d * 4 + 2 * tm * (rw + cw) * 2
                                  + tm * d * 4) / 2 ** 20 + 8),
        name="outproj_even",
    )(ret, cv, w_bf, xs, mod_l)


def _premix_odd_kernel(x_ref, mod_ref, g_ref, w_ref, o_ref, h_scr):
    j = pl.program_id(1)

    @pl.when(j == 0)
    def _():
        m = mod_ref[0]
        h_scr[...] = _norm_mod(x_ref[...], g_ref[...], m[0:1], m[1:2]).astype(BF16)

    z = jnp.dot(h_scr[...], w_ref[...], preferred_element_type=F32)
    o_ref[...] = (0.5 * z * (1.0 + lax.erf(z * (2.0 ** -0.5)))).astype(BF16)


def _premix_odd(xs, mod_l, g, w_bf, nb, n_lat, nrows):
    d = xs.shape[1]
    n_out = w_bf.shape[1]
    tm = _pick_tile(n_lat, nrows, (1024, 512, 256))
    tn = 1024
    bpb = n_lat // tm
    return pl.pallas_call(
        _premix_odd_kernel,
        grid=(nrows // tm, n_out // tn),
        in_specs=[pl.BlockSpec((tm, d), lambda i, j: (i, 0)),
                  pl.BlockSpec((1, 6, d), lambda i, j: (_group_of(i, bpb, nb), 0, 0)),
                  pl.BlockSpec((1, d), lambda i, j: (0, 0)),
                  pl.BlockSpec((d, tn), lambda i, j: (0, j))],
        out_specs=pl.BlockSpec((tm, tn), lambda i, j: (i, j)),
        out_shape=jax.ShapeDtypeStruct((nrows, n_out), BF16),
        scratch_shapes=[pltpu.VMEM((tm, d), BF16)],
        compiler_params=_cparams(("arbitrary", "arbitrary"),
                                 (2 * tm * d * 4 + tm * d * 2 + 2 * d * tn * 2 + 2 * tm * tn * 2
                                  + 2 * tm * tn * 4) / 2 ** 20 + 8),
        name="premix_odd",
    )(xs, mod_l, g.reshape(1, d), w_bf)


def _sgu_out_kernel(u_ref, v_ref, lg_ref, lb_ref, ws_ref, bs_ref, w_ref, x_ref, mod_ref, o_ref, p_scr):
    tm, sw = u_ref.shape
    gw = sw // SGU_GROUPS
    v = v_ref[...].astype(F32)
    mu = jnp.mean(v, axis=-1, keepdims=True)
    dlt = v - mu
    var = jnp.mean(dlt * dlt, axis=-1, keepdims=True)
    p_scr[...] = (dlt * lax.rsqrt(var + EPS) * lg_ref[...] + lb_ref[...]).astype(BF16)
    for cc in range(tm // SGU_CHUNK):
        rows = slice(cc * SGU_CHUNK, (cc + 1) * SGU_CHUNK)
        for gi in range(SGU_GROUPS):
            cols = slice(gi * gw, (gi + 1) * gw)
            mixed = jnp.dot(ws_ref[gi], p_scr[rows, cols], preferred_element_type=F32) + bs_ref[:, cols]
            p_scr[rows, cols] = (u_ref[rows, cols].astype(F32) * mixed).astype(BF16)
    y = jnp.dot(p_scr[...], w_ref[...], preferred_element_type=F32)
    o_ref[...] = x_ref[...] + mod_ref[0][2:3] * y


def _sgu_out(uv, ln_g, ln_b, ws_bf, bias_full, w_bf, xs, mod_l, nb, n_lat, nrows):
    nt, d = xs.shape
    sw = w_bf.shape[0]
    tm = _pick_tile(n_lat, nrows, (512, 256))
    bpb = n_lat // tm
    vec = lambda i: (0, 0)
    return pl.pallas_call(
        _sgu_out_kernel,
        grid=(nrows // tm,),
        in_specs=[pl.BlockSpec((tm, sw), lambda i: (i, 0)),
                  pl.BlockSpec((tm, sw), lambda i: (i, 1)),
                  pl.BlockSpec((1, sw), vec),
                  pl.BlockSpec((1, sw), vec),
                  pl.BlockSpec((SGU_GROUPS, SGU_CHUNK, SGU_CHUNK), lambda i: (0, 0, 0)),
                  pl.BlockSpec((SGU_CHUNK, sw), vec),
                  pl.BlockSpec((sw, d), vec),
                  pl.BlockSpec((tm, d), lambda i: (i, 0)),
                  pl.BlockSpec((1, 6, d), lambda i: (_group_of(i, bpb, nb), 0, 0))],
        out_specs=pl.BlockSpec((tm, d), lambda i: (i, 0)),
        out_shape=jax.ShapeDtypeStruct((nt, d), F32),
        scratch_shapes=[pltpu.VMEM((tm, sw), BF16)],
        input_output_aliases={7: 0},
        compiler_params=_cparams(("arbitrary",),
                                 (2 * sw * d * 2 + 4 * tm * d * 4 + 4 * tm * sw * 2 + tm * sw * 2
                                  + 2 * SGU_CHUNK * sw * 4 + 2 * tm * sw * 4) / 2 ** 20 + 8),
        name="sgu_out",
    )(uv, uv, ln_g.reshape(1, sw), ln_b.reshape(1, sw), ws_bf, bias_full, w_bf, xs, mod_l)


def _router_kernel(x_ref, mod_ref, g_ref, whi_ref, wlo_ref, rb_ref, h_ref, aux_ref, cnt_ref, before_ref, carry):
    i = pl.program_id(0)
    tm = x_ref.shape[0]

    @pl.when(i == 0)
    def _():
        carry[...] = jnp.zeros_like(carry)

    m = mod_ref[0]
    h = _norm_mod(x_ref[...], g_ref[...], m[3:4], m[4:5])
    h_ref[...] = h
    h_hi = h.astype(BF16)
    h_lo = (h - h_hi.astype(F32)).astype(BF16)
    logits = (jnp.dot(h_hi, whi_ref[...], preferred_element_type=F32)
              + jnp.dot(h_lo, whi_ref[...], preferred_element_type=F32)
              + jnp.dot(h_hi, wlo_ref[...], preferred_element_type=F32)) + rb_ref[...]

    lane = lax.broadcasted_iota(jnp.int32, (tm, LANES), 1).astype(F32)
    vals, idxs = [], []
    l = logits
    for _ in range(TOP_K):
        mx = jnp.max(l, axis=-1, keepdims=True)
        ix = jnp.min(jnp.where(l == mx, lane, float(LANES)), axis=-1, keepdims=True)
        vals.append(mx)
        idxs.append(ix)
        l = jnp.where(lane == ix, -jnp.inf, l)
    es = [jnp.exp(v - vals[0]) for v in vals]
    den = es[0]
    for e in es[1:]:
        den = den + e

    sel = jnp.zeros((tm, LANES), F32)
    for ix in idxs:
        sel = sel + jnp.where(lane == ix, 1.0, 0.0)
    rr = lax.broadcasted_iota(jnp.int32, (tm, tm), 0)
    rc = lax.broadcasted_iota(jnp.int32, (tm, tm), 1)
    ltri = jnp.where(rc < rr, 1.0, 0.0).astype(BF16)
    rank = jnp.dot(ltri, sel.astype(BF16), preferred_element_type=F32) + carry[...]
    before_ref[0] = jnp.broadcast_to(carry[...], before_ref.shape[1:])
    carry[...] = carry[...] + jnp.sum(sel, axis=0, keepdims=True)
    cnt_ref[...] = carry[...]

    aux = jnp.zeros((tm, LANES), F32)
    for r in range(TOP_K):
        rk = jnp.sum(jnp.where(lane == idxs[r], rank, 0.0), axis=-1, keepdims=True)
        aux = jnp.where(lane == float(r), idxs[r], aux)
        aux = jnp.where(lane == float(TOP_K + r), rk, aux)
        aux = jnp.where(lane == float(2 * TOP_K + r), es[r] / den, aux)
    aux_ref[...] = aux


def _router(xs, mod_l, g, whi, wlo, rb, nb, n_lat, nrows):
    d = xs.shape[1]
    tm = 256
    bpb = n_lat // tm
    vec = lambda i: (0, 0)
    return pl.pallas_call(
        _router_kernel,
        grid=(nrows // tm,),
        in_specs=[pl.BlockSpec((tm, d), lambda i: (i, 0)),
                  pl.BlockSpec((1, 6, d), lambda i: (_group_of(i, bpb, nb), 0, 0)),
                  pl.BlockSpec((1, d), vec),
                  pl.BlockSpec((d, LANES), vec),
                  pl.BlockSpec((d, LANES), vec),
                  pl.BlockSpec((1, LANES), vec)],
        out_specs=[pl.BlockSpec((tm, d), lambda i: (i, 0)),
                   pl.BlockSpec((tm, LANES), lambda i: (i, 0)),
                   pl.BlockSpec((1, LANES), vec),
                   pl.BlockSpec((1, 8, LANES), lambda i: (i, 0, 0))],
        out_shape=[jax.ShapeDtypeStruct((nrows, d), F32),
                   jax.ShapeDtypeStruct((nrows, LANES), F32),
                   jax.ShapeDtypeStruct((1, LANES), F32),
                   jax.ShapeDtypeStruct((nrows // tm, 8, LANES), F32)],
        scratch_shapes=[pltpu.VMEM((1, LANES), F32)],
        compiler_params=_cparams(("arbitrary",), 32),
        name="router",
    )(xs, mod_l, g.reshape(1, d), whi, wlo, rb)


def _dispatch_kernel(pos_ref, h_ref, hs_ref, sem):
    tm = h_ref.shape[0]

    def row_copy(t, p):
        return pltpu.make_async_copy(h_ref.at[pl.ds(t, 1), :], hs_ref.at[pl.ds(p, 1), :], sem)

    def issue(t, carry):
        for r in range(TOP_K):
            row_copy(t, pos_ref[0, 0, t * TOP_K + r]).start()
        return carry

    lax.fori_loop(0, tm, issue, 0)
    for _ in range(TOP_K):
        pltpu.make_async_copy(h_ref, hs_ref.at[pl.ds(0, tm), :], sem).wait()


def _dispatch(h, pos):
    nrows, d = h.shape
    tm = 256
    pos3 = pos.reshape(nrows // tm, 1, tm * TOP_K)
    return pl.pallas_call(
        _dispatch_kernel,
        grid=(nrows // tm,),
        in_specs=[pl.BlockSpec((1, 1, tm * TOP_K), lambda i: (i, 0, 0), memory_space=pltpu.SMEM),
                  pl.BlockSpec((tm, d), lambda i: (i, 0))],
        out_specs=pl.BlockSpec(memory_space=pl.ANY),
        out_shape=jax.ShapeDtypeStruct((nrows * TOP_K, d), F32),
        scratch_shapes=[pltpu.SemaphoreType.DMA(())],
        compiler_params=_cparams(("arbitrary",), 16),
        name="moe_dispatch",
    )(pos3, h)


def _pair_perm():
    i = jnp.arange(2 * LANES)[:, None]
    j = jnp.arange(2 * LANES)[None, :]
    return jnp.where(j < LANES, i == 2 * j, i == 2 * (j - LANES) + 1).astype(BF16)


def _prep_wgu_kernel(w_ref, p_ref, o_ref):
    blk = 2 * LANES
    for b in range(w_ref.shape[2] // blk):
        cols = slice(b * blk, (b + 1) * blk)
        o_ref[0, :, cols] = jnp.dot(w_ref[0, :, cols].astype(BF16), p_ref[...],
                                    preferred_element_type=F32).astype(BF16)


def _prep_wgu(wgu):
    lead = wgu.shape[:-2]
    d, n2 = wgu.shape[-2:]
    ne = math.prod(lead)
    tr = 1024 if d % 1024 == 0 else d
    out = pl.pallas_call(
        _prep_wgu_kernel,
        grid=(ne, d // tr),
        in_specs=[pl.BlockSpec((1, tr, n2), lambda e, r: (e, r, 0)),
                  pl.BlockSpec((2 * LANES, 2 * LANES), lambda e, r: (0, 0))],
        out_specs=pl.BlockSpec((1, tr, n2), lambda e, r: (e, r, 0)),
        out_shape=jax.ShapeDtypeStruct((ne, d, n2), BF16),
        compiler_params=_cparams(("arbitrary", "arbitrary"), 3 * tr * n2 * 4 / 2 ** 20 + 8),
        name="prep_wgu",
    )(wgu.reshape(ne, d, n2), _pair_perm())
    return out.reshape(lead + (d, n2))


def _expert_kernel(tile_ref, exp_ref, lo_ref, hi_ref, first_ref, hs_ref, wgu_ref, bgu_ref, wd_ref, bd_ref, o_ref):
    del tile_ref, exp_ref
    w = pl.program_id(0)
    tm = hs_ref.shape[0]
    z = jnp.dot(hs_ref[...].astype(BF16), wgu_ref[0], preferred_element_type=F32) + bgu_ref[0]
    parts = []
    for b in range(z.shape[1] // (2 * LANES)):
        zg = jnp.minimum(z[:, 2 * b * LANES:(2 * b + 1) * LANES], SWIGLU_LIMIT)
        zu = jnp.clip(z[:, (2 * b + 1) * LANES:(2 * b + 2) * LANES], -SWIGLU_LIMIT, SWIGLU_LIMIT)
        parts.append(((zu + 1.0) * (zg * jax.nn.sigmoid(SWIGLU_ALPHA * zg))).astype(BF16))
    act = jnp.concatenate(parts, axis=1)
    y = jnp.dot(act, wd_ref[0], preferred_element_type=F32) + bd_ref[0]
    row = lax.broadcasted_iota(jnp.int32, (tm, 1), 0)
    mine = (row >= lo_ref[w]) & (row < hi_ref[w])

    @pl.when(first_ref[w] == 1)
    def _():
        o_ref[...] = jnp.where(mine, y, 0.0)

    @pl.when(first_ref[w] == 0)
    def _():
        o_ref[...] = jnp.where(mine, y, o_ref[...])


def _experts(hs, items, wgu_bf, bgu, wd_bf, bd, tm, ebase):
    p, d = hs.shape
    n2 = wgu_bf.shape[2]
    ne = bgu.shape[0]
    de = n2 // 2
    nw = items[0].shape[0]
    grid_spec = pltpu.PrefetchScalarGridSpec(
        num_scalar_prefetch=5,
        grid=(nw,),
        in_specs=[pl.BlockSpec((tm, d), lambda w, t, e, lo, hi, f: (t[w], 0)),
                  pl.BlockSpec((1, d, n2), lambda w, t, e, lo, hi, f: (ebase + e[w], 0, 0)),
                  pl.BlockSpec((1, 1, n2), lambda w, t, e, lo, hi, f: (e[w], 0, 0)),
                  pl.BlockSpec((1, de, d), lambda w, t, e, lo, hi, f: (ebase + e[w], 0, 0)),
                  pl.BlockSpec((1, 1, d), lambda w, t, e, lo, hi, f: (e[w], 0, 0))],
        out_specs=pl.BlockSpec((tm, d), lambda w, t, e, lo, hi, f: (t[w], 0)),
    )
    return pl.pallas_call(
        _expert_kernel,
        grid_spec=grid_spec,
        out_shape=jax.ShapeDtypeStruct((p, d), F32),
        compiler_params=_cparams(("arbitrary",),
                                 (4 * tm * d * 4 + 2 * d * n2 * 2 + 2 * de * d * 2 + 2 * tm * n2 * 4
                                  + tm * d * 4) / 2 ** 20 + 6),
        name="moe_experts",
    )(*items, hs, wgu_bf, bgu.reshape(ne, 1, n2), wd_bf, bd.reshape(ne, 1, d))


def _expert_items(counts, p, tm, ne):
    ntiles = p // tm
    nw = ntiles + ne - 1
    off = jnp.concatenate([jnp.zeros((1,), jnp.int32), jnp.cumsum(counts)])
    first_tile = off[:-1] // tm
    last_tile = (off[1:] - 1) // tm
    ntile_e = jnp.where(counts > 0, last_tile - first_tile + 1, 0)
    base_incl = jnp.cumsum(ntile_e)
    base = base_incl - ntile_e
    total = base_incl[-1]
    w = jnp.arange(nw, dtype=jnp.int32)
    wc = jnp.minimum(w, total - 1)
    e = jnp.sum((base_incl[None, :] <= wc[:, None]).astype(jnp.int32), axis=1)
    t = first_tile[e] + (wc - base[e])
    lo = jnp.clip(off[e] - t * tm, 0, tm)
    hi = jnp.clip(off[e + 1] - t * tm, 0, tm)
    valid = w < total
    lo = jnp.where(valid, lo, 0)
    hi = jnp.where(valid, hi, 0)
    prev_t = jnp.concatenate([jnp.full((1,), -1, jnp.int32), t[:-1]])
    first = (valid & (t != prev_t)).astype(jnp.int32)
    return (t.astype(jnp.int32), e, lo.astype(jnp.int32), hi.astype(jnp.int32), first), off


def _combine_kernel(c0_ref, cc_ref, sb_ref, tot_ref, ys_ref, aux_ref, tab_ref, x_ref, mod_ref, fg_ref, o_ref,
                    buf, sem, *, final, ne, ntiles):
    i = pl.program_id(0)
    tm = x_ref.shape[0]
    nrow = buf.shape[1]
    slot = i % 2

    def fetch(tile, sl):
        def per_expert(e, carry):
            j = tile * ne + e

            def per_chunk(k, c):
                src_row = pl.multiple_of((c0_ref[j] + k) * CHUNK_ROWS, CHUNK_ROWS)
                dst_row = pl.multiple_of((sb_ref[j] + k) * CHUNK_ROWS, CHUNK_ROWS)
                pltpu.make_async_copy(ys_ref.at[pl.ds(src_row, CHUNK_ROWS), :],
                                      buf.at[sl, pl.ds(dst_row, CHUNK_ROWS), :], sem.at[sl]).start()
                return c
            return lax.fori_loop(0, cc_ref[j], per_chunk, carry)
        lax.fori_loop(0, ne, per_expert, 0)

    def wait_fetched(tile, sl):
        total = tot_ref[tile]
        for b in range((nrow // CHUNK_ROWS).bit_length()):
            rows = CHUNK_ROWS << b

            @pl.when((total >> b) & 1 == 1)
            def _():
                pltpu.make_async_copy(ys_ref.at[pl.ds(0, rows), :], buf.at[sl, pl.ds(0, rows), :],
                                      sem.at[sl]).wait()

    @pl.when(i == 0)
    def _():
        buf[...] = jnp.zeros_like(buf)
        fetch(0, 0)

    @pl.when(i + 1 < ntiles)
    def _():
        fetch(i + 1, 1 - slot)

    wait_fetched(i, slot)

    aux = aux_ref[...]
    tab = tab_ref[0]
    lane = lax.broadcasted_iota(jnp.int32, (tm, LANES), 1).astype(F32)
    col = lax.broadcasted_iota(jnp.int32, (tm, nrow), 1).astype(F32)
    gmat = jnp.zeros((tm, nrow), F32)
    for r in range(TOP_K):
        base = jnp.sum(jnp.where(lane == aux[:, r:r + 1], tab, 0.0), axis=-1, keepdims=True)
        lrow = base + aux[:, TOP_K + r:TOP_K + r + 1]
        gmat = gmat + jnp.where(col == lrow, aux[:, 2 * TOP_K + r:2 * TOP_K + r + 1], 0.0)
    y = jnp.dot(gmat.astype(BF16), buf[slot].astype(BF16), preferred_element_type=F32)
    xn = x_ref[...] + mod_ref[0][5:6] * y
    if final:
        ms = jnp.mean(xn * xn, axis=-1, keepdims=True)
        xn = xn * lax.rsqrt(ms + EPS) * fg_ref[...]
    o_ref[...] = xn


def _combine_plan(off, before, counts, ne):
    start = off[None, :ne] + before
    end = off[None, :ne] + jnp.concatenate([before[1:], counts[None, :]], axis=0)
    c0 = start // CHUNK_ROWS
    cc = jnp.where(end > start, (end + CHUNK_ROWS - 1) // CHUNK_ROWS - c0, 0)
    sb = jnp.cumsum(cc, axis=1) - cc
    tab = (sb - c0) * CHUNK_ROWS + off[None, :ne]
    tab = jnp.zeros((before.shape[0], 1, LANES), F32).at[:, 0, :ne].set(tab.astype(F32))
    flat = lambda a: a.reshape(-1).astype(jnp.int32)
    return flat(c0), flat(cc), flat(sb), flat(jnp.sum(cc, axis=1)), tab


def _combine(ys, plan, aux, xs, mod_l, final_g, nb, n_lat, nrows, final, ne):
    d = xs.shape[1]
    tm = 256
    bpb = n_lat // tm
    ntiles = nrows // tm
    c0, cc, sb, tot, tab = plan
    max_chunks = tm * TOP_K // CHUNK_ROWS + 2 * ne
    out_rows = nrows if final else xs.shape[0]
    grid_spec = pltpu.PrefetchScalarGridSpec(
        num_scalar_prefetch=4,
        grid=(ntiles,),
        in_specs=[pl.BlockSpec(memory_space=pl.ANY),
                  pl.BlockSpec((tm, LANES), lambda i, *_: (i, 0)),
                  pl.BlockSpec((1, 1, LANES), lambda i, *_: (i, 0, 0)),
                  pl.BlockSpec((tm, d), lambda i, *_: (i, 0)),
                  pl.BlockSpec((1, 6, d), lambda i, *_: (_group_of(i, bpb, nb), 0, 0)),
                  pl.BlockSpec((1, d), lambda i, *_: (0, 0))],
        out_specs=pl.BlockSpec((tm, d), lambda i, *_: (i, 0)),
        scratch_shapes=[pltpu.VMEM((2, max_chunks * CHUNK_ROWS, d), F32), pltpu.SemaphoreType.DMA((2,))],
    )
    return pl.pallas_call(
        functools.partial(_combine_kernel, final=final, ne=ne, ntiles=ntiles),
        grid_spec=grid_spec,
        out_shape=jax.ShapeDtypeStruct((out_rows, d), F32),
        input_output_aliases={} if final else {7: 0},
        compiler_params=_cparams(("arbitrary",),
                                 (3 * max_chunks * CHUNK_ROWS * d * 4 + 8 * tm * d * 4) / 2 ** 20 + 6),
        name="moe_combine",
    )(c0, cc, sb, tot, ys, aux, tab, xs, mod_l, final_g.reshape(1, d))


def _moe(xs, mod_l, g, rw, rb, wgu_bf, bgu, wd_bf, bd, ebase, final_g, nb, n_lat, nrows, final):
    d = xs.shape[1]
    ne = rw.shape[1]
    rw_pad = jnp.zeros((d, LANES), F32).at[:, :ne].set(rw.astype(F32))
    whi = rw_pad.astype(BF16)
    wlo = (rw_pad - whi.astype(F32)).astype(BF16)
    rb_pad = jnp.full((1, LANES), -1e30, F32).at[0, :ne].set(rb.astype(F32))
    h, aux, cnt, before = _router(xs, mod_l, g, whi, wlo, rb_pad, nb, n_lat, nrows)

    counts = cnt[0, :ne].astype(jnp.int32)
    before = before[:, 0, :ne].astype(jnp.int32)
    tm_e = 512
    p = nrows * TOP_K
    items, off = _expert_items(counts, p, tm_e, ne)
    idx = aux[:, 0:TOP_K].astype(jnp.int32)
    rank = aux[:, TOP_K:2 * TOP_K].astype(jnp.int32)
    pos = (off[idx] + rank).reshape(-1)

    hs = _dispatch(h, pos)
    n2 = bgu.shape[1]
    bgu_p = bgu.astype(F32).reshape(ne, n2 // (2 * LANES), LANES, 2).transpose(0, 1, 3, 2).reshape(ne, n2)
    ys = _experts(hs, items, wgu_bf, bgu_p, wd_bf, bd.astype(F32), tm_e, ebase)
    plan = _combine_plan(off, before, counts, ne)
    return _combine(ys, plan, aux, xs, mod_l, final_g, nb, n_lat, nrows, final, ne)


def _rope_tables(nb, n_lat, n_ctx):
    def angles(seq_pos, row_pos, col_pos):
        parts = []
        for posv, dim in zip((seq_pos, row_pos, col_pos), ROPE_AXES):
            inv = ROPE_THETA ** (-jnp.arange(0, dim, 2, dtype=F32) / dim)
            parts.append(posv.astype(F32)[:, None] * inv[None, :])
        return jnp.concatenate(parts, axis=-1)

    rows = n_lat // GRID_W
    lat = angles(jnp.full((n_lat,), n_ctx, jnp.int32),
                 jnp.repeat(jnp.arange(rows, dtype=jnp.int32), GRID_W),
                 jnp.tile(jnp.arange(GRID_W, dtype=jnp.int32), rows))
    zl = jnp.zeros((n_ctx,), jnp.int32)
    ctx = angles(jnp.arange(n_ctx, dtype=jnp.int32), zl, zl)
    ang = jnp.concatenate([lat, jnp.tile(ctx, (nb, 1))], axis=0)
    cos, sin = jnp.cos(ang), jnp.sin(ang)
    return jnp.concatenate([cos, cos], axis=-1), jnp.concatenate([-sin, sin], axis=-1)


def _deinterleave_heads(w):
    d, n = w.shape
    w4 = w.reshape(d, n // RET_DK, RET_DK // 2, 2)
    return jnp.concatenate([w4[..., 0], w4[..., 1]], axis=-1).reshape(d, n)


def kernel(x, c, ctx, c_ctx, mod_w, mod_b, norm_mix_g, norm_ffn_g, ev_w_in, ev_w_out, ret_decay_f, ret_decay_b,
           ret_gn_g, conv_w, conv_b, conv_ln_g, conv_ln_b, od_w_in, od_w_out, sgu_ln_g, sgu_ln_b, sgu_w, sgu_b,
           router_w, router_b, moe_w_gu, moe_b_gu, moe_w_down, moe_b_down, final_g):
    nb, n_lat, d = x.shape
    n_ctx = ctx.shape[1]
    depth = mod_w.shape[0]
    n_lat_total = nb * n_lat
    n_ctx_total = nb * n_ctx
    nt = n_lat_total + n_ctx_total
    assert nb < MOD_GROUPS and n_lat % GRID_W == 0
    qk = RET_HEADS * RET_DK

    xs = jnp.concatenate([x.reshape(n_lat_total, d), ctx.reshape(n_ctx_total, d)], axis=0).astype(F32)
    cc = jnp.zeros((MOD_GROUPS, d), F32).at[:nb].set(c.astype(F32)).at[nb].set(c_ctx.astype(F32))
    mod = _mod_table(cc, mod_w, mod_b).reshape(depth, MOD_GROUPS, 6, d)
    cos_t, sin_t = _rope_tables(nb, n_lat, n_ctx)
    ne = router_w.shape[2]
    wgu_bf = _prep_wgu(moe_w_gu.reshape((depth * ne,) + moe_w_gu.shape[2:]))
    wd_bf = moe_w_down.reshape((depth * ne,) + moe_w_down.shape[2:]).astype(BF16)

    out = None
    for layer in range(depth):
        last = layer == depth - 1
        nrows = n_lat_total if last else nt
        mod_l = mod[layer]
        if layer % 2 == 0:
            e = layer // 2
            w_in = ev_w_in[e]
            w_in = jnp.concatenate([_deinterleave_heads(w_in[:, :qk]), _deinterleave_heads(w_in[:, qk:2 * qk]),
                                    w_in[:, 2 * qk:]], axis=1).astype(BF16)
            z = _premix_even(xs, mod_l, norm_mix_g[layer], w_in, cos_t, sin_t, nb, n_lat, n_ctx_total)
            lg = jnp.stack([-jnp.exp(ret_decay_f[e].astype(F32)), -jnp.exp(ret_decay_b[e].astype(F32))])
            gn = ret_gn_g[e].reshape(1, qk).astype(F32)
            s0 = jnp.zeros((nb, RET_HEADS, 2, RET_DK, RET_DK), F32)
            ret0 = jnp.zeros((nt, qk), BF16)
            ret, st = _retention(z, lg, gn, s0, ret0, nb, n_ctx, n_lat_total)
            ret, _ = _retention(z, lg, gn, st, ret, nb, n_lat, 0)
            cv = _conv_module(z, conv_w[e].astype(F32), conv_b[e].astype(F32), conv_ln_g[e].astype(F32),
                              conv_ln_b[e].astype(F32), nb, n_lat, n_ctx)
            xs = _outproj_even(ret, cv, ev_w_out[e].astype(BF16), xs, mod_l, nb, n_lat, n_ctx_total)
        else:
            j = layer // 2
            uv = _premix_odd(xs, mod_l, norm_mix_g[layer], od_w_in[j].astype(BF16), nb, n_lat, nrows)
            sw = od_w_out.shape[1]
            bias_full = jnp.repeat(sgu_b[j].astype(F32).T, sw // SGU_GROUPS, axis=1)
            xs = _sgu_out(uv, sgu_ln_g[j].astype(F32), sgu_ln_b[j].astype(F32), sgu_w[j].astype(BF16), bias_full,
                          od_w_out[j].astype(BF16), xs, mod_l, nb, n_lat, nrows)
        res = _moe(xs, mod_l, norm_ffn_g[layer], router_w[layer], router_b[layer], wgu_bf, moe_b_gu[layer],
                   wd_bf, moe_b_down[layer], layer * ne, final_g.astype(F32), nb, n_lat, nrows, last)
        if last:
            out = res
        else:
            xs = res
    return out.reshape(nb, n_lat, d).astype(x.dtype)
```

```python
import functools
import math

import jax
import jax.numpy as jnp
from jax import lax
from jax.experimental import pallas as pl
from jax.experimental.pallas import tpu as pltpu

F32 = jnp.float32
BF16 = jnp.bfloat16

LANES = 128
CHUNK_ROWS = 16
RET_HEADS = 8
RET_DK = 128
RET_CHUNK = 128
ROPE_AXES = (32, 48, 48)
ROPE_THETA = 10000.0
GRID_W = 64
SGU_GROUPS = 8
SGU_CHUNK = 128
TOP_K = 4
SWIGLU_LIMIT = 7.0
SWIGLU_ALPHA = 1.702
EPS = 1e-6
MOD_GROUPS = 8
HALO = 16
VMEM_CAP = 58 * 2 ** 20


def _cparams(sem, vmem_mb):
    return pltpu.CompilerParams(dimension_semantics=sem,
                                vmem_limit_bytes=min(int(vmem_mb * 2 ** 20), VMEM_CAP))


def _pick_tile(n_lat, n_ctx, cands):
    for t in cands:
        if n_lat % t == 0 and n_ctx % t == 0:
            return t
    raise ValueError("no token tile divides the latent and context lengths")


def _group_of(i, blocks_per_batch, nb):
    return jnp.minimum(i // blocks_per_batch, nb)


def _norm_mod(x, g, shift, scale):
    ms = jnp.mean(x * x, axis=-1, keepdims=True)
    return (x * lax.rsqrt(ms + EPS) * g) * (1.0 + scale) + shift


def _mod_kernel(c_ref, w_ref, b_ref, o_ref):
    c = c_ref[...]
    s = (c * jax.nn.sigmoid(c)).astype(BF16)
    o_ref[0] = jnp.dot(s, w_ref[0].astype(BF16), preferred_element_type=F32) + b_ref[0]


def _mod_table(cc, mod_w, mod_b):
    depth, d, n6 = mod_w.shape
    tn = min(d, 1024)
    return pl.pallas_call(
        _mod_kernel,
        grid=(depth, n6 // tn),
        in_specs=[pl.BlockSpec((MOD_GROUPS, d), lambda l, j: (0, 0)),
                  pl.BlockSpec((1, d, tn), lambda l, j: (l, 0, j)),
                  pl.BlockSpec((1, 1, tn), lambda l, j: (l, 0, j))],
        out_specs=pl.BlockSpec((1, MOD_GROUPS, tn), lambda l, j: (l, 0, j)),
        out_shape=jax.ShapeDtypeStruct((depth, MOD_GROUPS, n6), F32),
        compiler_params=_cparams(("arbitrary", "arbitrary"), 2 * d * tn * 4 / 2 ** 20 + 8),
        name="mod_table",
    )(cc, mod_w, mod_b.reshape(depth, 1, n6))


def _premix_even_kernel(x_ref, mod_ref, g_ref, w_ref, cos_ref, sin_ref, o_ref, h_scr, *, kscale):
    j = pl.program_id(1)

    @pl.when(j == 0)
    def _():
        m = mod_ref[0]
        h_scr[...] = _norm_mod(x_ref[...], g_ref[...], m[0:1], m[1:2]).astype(BF16)

    z = jnp.dot(h_scr[...], w_ref[...], preferred_element_type=F32)

    @pl.when(j < 2)
    def _():
        c = cos_ref[...]
        s = sin_ref[...]
        scale = jnp.where(j == 1, kscale, 1.0).astype(F32)
        for hh in range(z.shape[1] // RET_DK):
            t = z[:, hh * RET_DK:(hh + 1) * RET_DK]
            r = pltpu.roll(t, RET_DK // 2, axis=1)
            o_ref[:, hh * RET_DK:(hh + 1) * RET_DK] = ((t * c + r * s) * scale).astype(BF16)

    @pl.when(j >= 2)
    def _():
        o_ref[...] = z.astype(BF16)


def _premix_even(xs, mod_l, g, w_bf, cos_t, sin_t, nb, n_lat, n_ctx_total):
    nt, d = xs.shape
    n_out = w_bf.shape[1]
    tm = _pick_tile(n_lat, n_ctx_total, (1024, 512, 256))
    tn = RET_HEADS * RET_DK
    bpb = n_lat // tm
    n_lat_blocks = nb * bpb

    def tab_idx(i, j):
        return (jnp.where(i < n_lat_blocks, i % bpb, bpb + (i - n_lat_blocks)), 0)

    return pl.pallas_call(
        functools.partial(_premix_even_kernel, kscale=RET_DK ** -0.5),
        grid=(nt // tm, n_out // tn),
        in_specs=[pl.BlockSpec((tm, d), lambda i, j: (i, 0)),
                  pl.BlockSpec((1, 6, d), lambda i, j: (_group_of(i, bpb, nb), 0, 0)),
                  pl.BlockSpec((1, d), lambda i, j: (0, 0)),
                  pl.BlockSpec((d, tn), lambda i, j: (0, j)),
                  pl.BlockSpec((tm, RET_DK), tab_idx),
                  pl.BlockSpec((tm, RET_DK), tab_idx)],
        out_specs=pl.BlockSpec((tm, tn), lambda i, j: (i, j)),
        out_shape=jax.ShapeDtypeStruct((nt, n_out), BF16),
        scratch_shapes=[pltpu.VMEM((tm, d), BF16)],
        compiler_params=_cparams(("arbitrary", "arbitrary"),
                                 (2 * tm * d * 4 + tm * d * 2 + 2 * d * tn * 2 + 2 * tm * tn * 2
                                  + 2 * tm * tn * 4) / 2 ** 20 + 8),
        name="premix_even",
    )(xs, mod_l, g.reshape(1, d), w_bf, cos_t, sin_t)


def _retention_kernel(lg_ref, q_ref, k_ref, v_ref, g_ref, gn_ref, s0_ref, ret_in_ref, ret_ref, sfin_ref,
                      st_scr, *, nc, unroll):
    del ret_in_ref
    hh = pl.program_id(1)
    lgf = lg_ref[0, hh]
    lgb = lg_ref[1, hh]
    c = RET_CHUNK
    ri = lax.broadcasted_iota(jnp.int32, (c, c), 0).astype(F32)
    ci = lax.broadcasted_iota(jnp.int32, (c, c), 1).astype(F32)
    rel = ri - ci
    dmat = jnp.where(rel >= 0, jnp.exp(lgf * jnp.maximum(rel, 0.0)), jnp.exp(lgb * jnp.maximum(-rel, 0.0)))
    xi_f = jnp.exp(lgf * (ri + 1.0))
    xi_b = jnp.exp(lgb * (c - ri))
    zeta_f = jnp.exp(lgf * (c - 1.0 - ci))
    zeta_b = jnp.exp(lgb * ci)
    gc_f = jnp.exp(lgf * c + jnp.zeros((c, c), F32))
    gc_b = jnp.exp(lgb * c + jnp.zeros((c, c), F32))

    def local_sums(i, carry):
        for u in range(unroll):
            cc = i * unroll + u
            r0 = pl.multiple_of(cc * c, c)
            kt = k_ref[pl.ds(r0, c), :].astype(F32).T
            v = v_ref[pl.ds(r0, c), :]
            st_scr[cc, :, 0:c] = jnp.dot((kt * zeta_f).astype(BF16), v, preferred_element_type=F32)
            st_scr[cc, :, c:2 * c] = jnp.dot((kt * zeta_b).astype(BF16), v, preferred_element_type=F32)
        return carry

    lax.fori_loop(0, nc // unroll, local_sums, 0)

    def scans(i, carry):
        sf, sb = carry
        cb = nc - 1 - i
        uf = st_scr[i, :, 0:c]
        st_scr[i, :, 0:c] = sf
        ub = st_scr[cb, :, c:2 * c]
        st_scr[cb, :, c:2 * c] = sb
        return gc_f * sf + uf, gc_b * sb + ub

    sf, sb = lax.fori_loop(0, nc, scans, (s0_ref[0, 0, 0], s0_ref[0, 0, 1]))
    sfin_ref[0, 0, 0] = sf
    sfin_ref[0, 0, 1] = sb

    def outputs(i, carry):
        for u in range(unroll):
            cc = i * unroll + u
            r0 = pl.multiple_of(cc * c, c)
            q = q_ref[pl.ds(r0, c), :]
            a = lax.dot_general(q, k_ref[pl.ds(r0, c), :], (((1,), (1,)), ((), ())), preferred_element_type=F32)
            o = jnp.dot((a * dmat).astype(BF16), v_ref[pl.ds(r0, c), :], preferred_element_type=F32)
            cr = jnp.dot(q, st_scr[cc].astype(BF16), preferred_element_type=F32)
            o = o + cr[:, 0:c] * xi_f + cr[:, c:2 * c] * xi_b
            mu = jnp.mean(o, axis=-1, keepdims=True)
            dlt = o - mu
            var = jnp.mean(dlt * dlt, axis=-1, keepdims=True)
            y = dlt * lax.rsqrt(var + EPS) * gn_ref[...]
            g = g_ref[pl.ds(r0, c), :].astype(F32)
            ret_ref[pl.ds(r0, c), :] = (g * jax.nn.sigmoid(g) * y).astype(BF16)
        return carry

    lax.fori_loop(0, nc // unroll, outputs, 0)


def _retention(z, lg, gn, s0, ret_prev, nb, seq, row0):
    nt = z.shape[0]
    nc = seq // RET_CHUNK
    rb0 = row0 // seq
    hcols = RET_HEADS

    def col(off):
        return lambda b, h: (rb0 + b, off * hcols + h)

    ret, sfin = pl.pallas_call(
        functools.partial(_retention_kernel, nc=nc, unroll=max(u for u in (8, 4, 2, 1) if nc % u == 0)),
        grid=(nb, RET_HEADS),
        in_specs=[pl.BlockSpec(memory_space=pltpu.SMEM),
                  pl.BlockSpec((seq, RET_DK), col(0)),
                  pl.BlockSpec((seq, RET_DK), col(1)),
                  pl.BlockSpec((seq, RET_DK), col(2)),
                  pl.BlockSpec((seq, RET_DK), col(3)),
                  pl.BlockSpec((1, RET_DK), lambda b, h: (0, h)),
                  pl.BlockSpec((1, 1, 2, RET_DK, RET_DK), lambda b, h: (b, h, 0, 0, 0)),
                  pl.BlockSpec(memory_space=pl.ANY)],
        out_specs=[pl.BlockSpec((seq, RET_DK), lambda b, h: (rb0 + b, h)),
                   pl.BlockSpec((1, 1, 2, RET_DK, RET_DK), lambda b, h: (b, h, 0, 0, 0))],
        out_shape=[jax.ShapeDtypeStruct((nt, RET_HEADS * RET_DK), BF16),
                   jax.ShapeDtypeStruct((nb, RET_HEADS, 2, RET_DK, RET_DK), F32)],
        scratch_shapes=[pltpu.VMEM((nc, RET_DK, 2 * RET_DK), F32)],
        input_output_aliases={7: 0},
        compiler_params=_cparams(("arbitrary", "arbitrary"),
                                 (10 * seq * RET_DK * 2 + 2 * nc * RET_DK * RET_DK * 4) / 2 ** 20 + 8),
        name="retention",
    )(lg, z, z, z, z, gn, s0, ret_prev)
    return ret, sfin


def _conv_kernel(a_ref, b_ref, ap_ref, bp_ref, an_ref, bn_ref, w_ref, cb_ref, lg_ref, lb_ref, o_ref, ext, shifted,
                 *, tm, n_lat_total, n_lat, n_ctx, ktaps):
    i = pl.program_id(0)
    row0 = i * tm
    is_lat = row0 < n_lat_total
    pos = jnp.where(is_lat, row0 % n_lat, (row0 - n_lat_total) % n_ctx)
    seq = jnp.where(is_lat, n_lat, n_ctx)
    keep_prev = (pos != 0).astype(F32)
    keep_next = (pos + tm != seq).astype(F32)

    def glu(a, b):
        return a.astype(F32) * jax.nn.sigmoid(b.astype(F32))

    ext[0:HALO, :] = glu(ap_ref[...], bp_ref[...]) * keep_prev
    ext[HALO:HALO + tm, :] = glu(a_ref[...], b_ref[...])
    ext[HALO + tm:HALO + tm + HALO, :] = glu(an_ref[...], bn_ref[...]) * keep_next

    half = ktaps // 2
    rc = 32
    sub = 8
    span = shifted.shape[1]
    for s in range(1, sub):
        shifted[s - 1] = ext[s:s + span, :]
    for r in range(tm // rc):
        acc = jnp.zeros((rc, a_ref.shape[1]), F32)
        for j in range(ktaps):
            o = HALO + r * rc + j - half
            s, base = o % sub, o - o % sub
            win = ext[base:base + rc, :] if s == 0 else shifted[s - 1, base:base + rc, :]
            acc = acc + w_ref[j:j + 1, :] * win
        y = acc + cb_ref[...]
        mu = jnp.mean(y, axis=-1, keepdims=True)
        dlt = y - mu
        var = jnp.mean(dlt * dlt, axis=-1, keepdims=True)
        yn = dlt * lax.rsqrt(var + EPS) * lg_ref[...] + lb_ref[...]
        o_ref[r * rc:(r + 1) * rc, :] = (yn * jax.nn.sigmoid(yn)).astype(BF16)


def _conv_module(z, conv_w, conv_b, ln_g, ln_b, nb, n_lat, n_ctx):
    nt = z.shape[0]
    ktaps, cw = conv_w.shape
    tm = 256
    assert n_lat % tm == 0 and n_ctx % tm == 0 and ktaps // 2 <= HALO
    a_blk = (2 * RET_HEADS * RET_DK + 2 * RET_HEADS * RET_DK) // cw
    hb = tm // HALO
    last = nt // HALO - 1
    kern = functools.partial(_conv_kernel, tm=tm, n_lat_total=nb * n_lat, n_lat=n_lat, n_ctx=n_ctx, ktaps=ktaps)
    vec = lambda i: (0, 0)
    return pl.pallas_call(
        kern,
        grid=(nt // tm,),
        in_specs=[pl.BlockSpec((tm, cw), lambda i: (i, a_blk)),
                  pl.BlockSpec((tm, cw), lambda i: (i, a_blk + 1)),
                  pl.BlockSpec((HALO, cw), lambda i: (jnp.maximum(i * hb - 1, 0), a_blk)),
                  pl.BlockSpec((HALO, cw), lambda i: (jnp.maximum(i * hb - 1, 0), a_blk + 1)),
                  pl.BlockSpec((HALO, cw), lambda i: (jnp.minimum((i + 1) * hb, last), a_blk)),
                  pl.BlockSpec((HALO, cw), lambda i: (jnp.minimum((i + 1) * hb, last), a_blk + 1)),
                  pl.BlockSpec((ktaps, cw), vec),
                  pl.BlockSpec((1, cw), vec),
                  pl.BlockSpec((1, cw), vec),
                  pl.BlockSpec((1, cw), vec)],
        out_specs=pl.BlockSpec((tm, cw), lambda i: (i, 0)),
        out_shape=jax.ShapeDtypeStruct((nt, cw), BF16),
        scratch_shapes=[pltpu.VMEM((tm + 2 * HALO, cw), F32),
                        pltpu.VMEM((7, tm + 2 * HALO - 8, cw), F32)],
        compiler_params=_cparams(("arbitrary",), 40),
        name="conv_module",
    )(z, z, z, z, z, z, conv_w, conv_b.reshape(1, cw), ln_g.reshape(1, cw), ln_b.reshape(1, cw))


def _outproj_even_kernel(ret_ref, cv_ref, w_ref, x_ref, mod_ref, o_ref):
    rw = ret_ref.shape[1]
    y = jnp.dot(ret_ref[...], w_ref[0:rw, :], preferred_element_type=F32)
    y = y + jnp.dot(cv_ref[...], w_ref[rw:, :], preferred_element_type=F32)
    o_ref[...] = x_ref[...] + mod_ref[0][2:3] * y


def _outproj_even(ret, cv, w_bf, xs, mod_l, nb, n_lat, n_ctx_total):
    nt, d = xs.shape
    tm = _pick_tile(n_lat, n_ctx_total, (512, 256))
    bpb = n_lat // tm
    rw, cw = ret.shape[1], cv.shape[1]
    return pl.pallas_call(
        _outproj_even_kernel,
        grid=(nt // tm,),
        in_specs=[pl.BlockSpec((tm, rw), lambda i: (i, 0)),
                  pl.BlockSpec((tm, cw), lambda i: (i, 0)),
                  pl.BlockSpec((rw + cw, d), lambda i: (0, 0)),
                  pl.BlockSpec((tm, d), lambda i: (i, 0)),
                  pl.BlockSpec((1, 6, d), lambda i: (_group_of(i, bpb, nb), 0, 0))],
        out_specs=pl.BlockSpec((tm, d), lambda i: (i, 0)),
        out_shape=jax.ShapeDtypeStruct((nt, d), F32),
        input_output_aliases={3: 0},
        compiler_params=_cparams(("arbitrary",),
                                 (2 * (rw + cw) * d * 2 + 4 * tm * d * 4 + 2 * tm * (rw + cw) * 2
                                  + tm * d * 4) / 2 ** 20 + 8),
        name="outproj_even",
    )(ret, cv, w_bf, xs, mod_l)


def _premix_odd_kernel(x_ref, mod_ref, g_ref, w_ref, o_ref, h_scr):
    j = pl.program_id(1)

    @pl.when(j == 0)
    def _():
        m = mod_ref[0]
        h_scr[...] = _norm_mod(x_ref[...], g_ref[...], m[0:1], m[1:2]).astype(BF16)

    z = jnp.dot(h_scr[...], w_ref[...], preferred_element_type=F32)
    o_ref[...] = (0.5 * z * (1.0 + lax.erf(z * (2.0 ** -0.5)))).astype(BF16)


def _premix_odd(xs, mod_l, g, w_bf, nb, n_lat, nrows):
    d = xs.shape[1]
    n_out = w_bf.shape[1]
    tm = _pick_tile(n_lat, nrows, (1024, 512, 256))
    tn = 1024
    bpb = n_lat // tm
    return pl.pallas_call(
        _premix_odd_kernel,
        grid=(nrows // tm, n_out // tn),
        in_specs=[pl.BlockSpec((tm, d), lambda i, j: (i, 0)),
                  pl.BlockSpec((1, 6, d), lambda i, j: (_group_of(i, bpb, nb), 0, 0)),
                  pl.BlockSpec((1, d), lambda i, j: (0, 0)),
                  pl.BlockSpec((d, tn), lambda i, j: (0, j))],
        out_specs=pl.BlockSpec((tm, tn), lambda i, j: (i, j)),
        out_shape=jax.ShapeDtypeStruct((nrows, n_out), BF16),
        scratch_shapes=[pltpu.VMEM((tm, d), BF16)],
        compiler_params=_cparams(("arbitrary", "arbitrary"),
                                 (2 * tm * d * 4 + tm * d * 2 + 2 * d * tn * 2 + 2 * tm * tn * 2
                                  + 2 * tm * tn * 4) / 2 ** 20 + 8),
        name="premix_odd",
    )(xs, mod_l, g.reshape(1, d), w_bf)


def _sgu_out_kernel(u_ref, v_ref, lg_ref, lb_ref, ws_ref, bs_ref, w_ref, x_ref, mod_ref, o_ref, p_scr):
    tm, sw = u_ref.shape
    gw = sw // SGU_GROUPS
    v = v_ref[...].astype(F32)
    mu = jnp.mean(v, axis=-1, keepdims=True)
    dlt = v - mu
    var = jnp.mean(dlt * dlt, axis=-1, keepdims=True)
    p_scr[...] = (dlt * lax.rsqrt(var + EPS) * lg_ref[...] + lb_ref[...]).astype(BF16)
    for cc in range(tm // SGU_CHUNK):
        rows = slice(cc * SGU_CHUNK, (cc + 1) * SGU_CHUNK)
        for gi in range(SGU_GROUPS):
            cols = slice(gi * gw, (gi + 1) * gw)
            mixed = jnp.dot(ws_ref[gi], p_scr[rows, cols], preferred_element_type=F32) + bs_ref[:, cols]
            p_scr[rows, cols] = (u_ref[rows, cols].astype(F32) * mixed).astype(BF16)
    y = jnp.dot(p_scr[...], w_ref[...], preferred_element_type=F32)
    o_ref[...] = x_ref[...] + mod_ref[0][2:3] * y


def _sgu_out(uv, ln_g, ln_b, ws_bf, bias_full, w_bf, xs, mod_l, nb, n_lat, nrows):
    nt, d = xs.shape
    sw = w_bf.shape[0]
    tm = _pick_tile(n_lat, nrows, (512, 256))
    bpb = n_lat // tm
    vec = lambda i: (0, 0)
    return pl.pallas_call(
        _sgu_out_kernel,
        grid=(nrows // tm,),
        in_specs=[pl.BlockSpec((tm, sw), lambda i: (i, 0)),
                  pl.BlockSpec((tm, sw), lambda i: (i, 1)),
                  pl.BlockSpec((1, sw), vec),
                  pl.BlockSpec((1, sw), vec),
                  pl.BlockSpec((SGU_GROUPS, SGU_CHUNK, SGU_CHUNK), lambda i: (0, 0, 0)),
                  pl.BlockSpec((SGU_CHUNK, sw), vec),
                  pl.BlockSpec((sw, d), vec),
                  pl.BlockSpec((tm, d), lambda i: (i, 0)),
                  pl.BlockSpec((1, 6, d), lambda i: (_group_of(i, bpb, nb), 0, 0))],
        out_specs=pl.BlockSpec((tm, d), lambda i: (i, 0)),
        out_shape=jax.ShapeDtypeStruct((nt, d), F32),
        scratch_shapes=[pltpu.VMEM((tm, sw), BF16)],
        input_output_aliases={7: 0},
        compiler_params=_cparams(("arbitrary",),
                                 (2 * sw * d * 2 + 4 * tm * d * 4 + 4 * tm * sw * 2 + tm * sw * 2
                                  + 2 * SGU_CHUNK * sw * 4 + 2 * tm * sw * 4) / 2 ** 20 + 8),
        name="sgu_out",
    )(uv, uv, ln_g.reshape(1, sw), ln_b.reshape(1, sw), ws_bf, bias_full, w_bf, xs, mod_l)


def _router_kernel(x_ref, mod_ref, g_ref, whi_ref, wlo_ref, rb_ref, h_ref, aux_ref, cnt_ref, before_ref, carry):
    i = pl.program_id(0)
    tm = x_ref.shape[0]

    @pl.when(i == 0)
    def _():
        carry[...] = jnp.zeros_like(carry)

    m = mod_ref[0]
    h = _norm_mod(x_ref[...], g_ref[...], m[3:4], m[4:5])
    h_ref[...] = h
    h_hi = h.astype(BF16)
    h_lo = (h - h_hi.astype(F32)).astype(BF16)
    logits = (jnp.dot(h_hi, whi_ref[...], preferred_element_type=F32)
              + jnp.dot(h_lo, whi_ref[...], preferred_element_type=F32)
              + jnp.dot(h_hi, wlo_ref[...], preferred_element_type=F32)) + rb_ref[...]

    lane = lax.broadcasted_iota(jnp.int32, (tm, LANES), 1).astype(F32)
    vals, idxs = [], []
    l = logits
    for _ in range(TOP_K):
        mx = jnp.max(l, axis=-1, keepdims=True)
        ix = jnp.min(jnp.where(l == mx, lane, float(LANES)), axis=-1, keepdims=True)
        vals.append(mx)
        idxs.append(ix)
        l = jnp.where(lane == ix, -jnp.inf, l)
    es = [jnp.exp(v - vals[0]) for v in vals]
    den = es[0]
    for e in es[1:]:
        den = den + e

    sel = jnp.zeros((tm, LANES), F32)
    for ix in idxs:
        sel = sel + jnp.where(lane == ix, 1.0, 0.0)
    rr = lax.broadcasted_iota(jnp.int32, (tm, tm), 0)
    rc = lax.broadcasted_iota(jnp.int32, (tm, tm), 1)
    ltri = jnp.where(rc < rr, 1.0, 0.0).astype(BF16)
    rank = jnp.dot(ltri, sel.astype(BF16), preferred_element_type=F32) + carry[...]
    before_ref[0] = jnp.broadcast_to(carry[...], before_ref.shape[1:])
    carry[...] = carry[...] + jnp.sum(sel, axis=0, keepdims=True)
    cnt_ref[...] = carry[...]

    aux = jnp.zeros((tm, LANES), F32)
    for r in range(TOP_K):
        rk = jnp.sum(jnp.where(lane == idxs[r], rank, 0.0), axis=-1, keepdims=True)
        aux = jnp.where(lane == float(r), idxs[r], aux)
        aux = jnp.where(lane == float(TOP_K + r), rk, aux)
        aux = jnp.where(lane == float(2 * TOP_K + r), es[r] / den, aux)
    aux_ref[...] = aux


def _router(xs, mod_l, g, whi, wlo, rb, nb, n_lat, nrows):
    d = xs.shape[1]
    tm = 256
    bpb = n_lat // tm
    vec = lambda i: (0, 0)
    return pl.pallas_call(
        _router_kernel,
        grid=(nrows // tm,),
        in_specs=[pl.BlockSpec((tm, d), lambda i: (i, 0)),
                  pl.BlockSpec((1, 6, d), lambda i: (_group_of(i, bpb, nb), 0, 0)),
                  pl.BlockSpec((1, d), vec),
                  pl.BlockSpec((d, LANES), vec),
                  pl.BlockSpec((d, LANES), vec),
                  pl.BlockSpec((1, LANES), vec)],
        out_specs=[pl.BlockSpec((tm, d), lambda i: (i, 0)),
                   pl.BlockSpec((tm, LANES), lambda i: (i, 0)),
                   pl.BlockSpec((1, LANES), vec),
                   pl.BlockSpec((1, 8, LANES), lambda i: (i, 0, 0))],
        out_shape=[jax.ShapeDtypeStruct((nrows, d), F32),
                   jax.ShapeDtypeStruct((nrows, LANES), F32),
                   jax.ShapeDtypeStruct((1, LANES), F32),
                   jax.ShapeDtypeStruct((nrows // tm, 8, LANES), F32)],
        scratch_shapes=[pltpu.VMEM((1, LANES), F32)],
        compiler_params=_cparams(("arbitrary",), 32),
        name="router",
    )(xs, mod_l, g.reshape(1, d), whi, wlo, rb)


def _dispatch_kernel(pos_ref, h_ref, hs_ref, sem):
    tm = h_ref.shape[0]

    def row_copy(t, p):
        return pltpu.make_async_copy(h_ref.at[pl.ds(t, 1), :], hs_ref.at[pl.ds(p, 1), :], sem)

    def issue(t, carry):
        for r in range(TOP_K):
            row_copy(t, pos_ref[0, 0, t * TOP_K + r]).start()
        return carry

    lax.fori_loop(0, tm, issue, 0)
    for _ in range(TOP_K):
        pltpu.make_async_copy(h_ref, hs_ref.at[pl.ds(0, tm), :], sem).wait()


def _dispatch(h, pos):
    nrows, d = h.shape
    tm = 256
    pos3 = pos.reshape(nrows // tm, 1, tm * TOP_K)
    return pl.pallas_call(
        _dispatch_kernel,
        grid=(nrows // tm,),
        in_specs=[pl.BlockSpec((1, 1, tm * TOP_K), lambda i: (i, 0, 0), memory_space=pltpu.SMEM),
                  pl.BlockSpec((tm, d), lambda i: (i, 0))],
        out_specs=pl.BlockSpec(memory_space=pl.ANY),
        out_shape=jax.ShapeDtypeStruct((nrows * TOP_K, d), F32),
        scratch_shapes=[pltpu.SemaphoreType.DMA(())],
        compiler_params=_cparams(("arbitrary",), 16),
        name="moe_dispatch",
    )(pos3, h)


def _pair_perm():
    i = jnp.arange(2 * LANES)[:, None]
    j = jnp.arange(2 * LANES)[None, :]
    return jnp.where(j < LANES, i == 2 * j, i == 2 * (j - LANES) + 1).astype(BF16)


def _prep_wgu_kernel(w_ref, p_ref, o_ref):
    blk = 2 * LANES
    for b in range(w_ref.shape[2] // blk):
        cols = slice(b * blk, (b + 1) * blk)
        o_ref[0, :, cols] = jnp.dot(w_ref[0, :, cols].astype(BF16), p_ref[...],
                                    preferred_element_type=F32).astype(BF16)


def _prep_wgu(wgu):
    lead = wgu.shape[:-2]
    d, n2 = wgu.shape[-2:]
    ne = math.prod(lead)
    tr = 1024 if d % 1024 == 0 else d
    out = pl.pallas_call(
        _prep_wgu_kernel,
        grid=(ne, d // tr),
        in_specs=[pl.BlockSpec((1, tr, n2), lambda e, r: (e, r, 0)),
                  pl.BlockSpec((2 * LANES, 2 * LANES), lambda e, r: (0, 0))],
        out_specs=pl.BlockSpec((1, tr, n2), lambda e, r: (e, r, 0)),
        out_shape=jax.ShapeDtypeStruct((ne, d, n2), BF16),
        compiler_params=_cparams(("arbitrary", "arbitrary"), 3 * tr * n2 * 4 / 2 ** 20 + 8),
        name="prep_wgu",
    )(wgu.reshape(ne, d, n2), _pair_perm())
    return out.reshape(lead + (d, n2))


def _expert_kernel(tile_ref, exp_ref, lo_ref, hi_ref, first_ref, hs_ref, wgu_ref, bgu_ref, wd_ref, bd_ref, o_ref):
    del tile_ref, exp_ref
    w = pl.program_id(0)
    tm = hs_ref.shape[0]
    z = jnp.dot(hs_ref[...].astype(BF16), wgu_ref[0], preferred_element_type=F32) + bgu_ref[0]
    parts = []
    for b in range(z.shape[1] // (2 * LANES)):
        zg = jnp.minimum(z[:, 2 * b * LANES:(2 * b + 1) * LANES], SWIGLU_LIMIT)
        zu = jnp.clip(z[:, (2 * b + 1) * LANES:(2 * b + 2) * LANES], -SWIGLU_LIMIT, SWIGLU_LIMIT)
        parts.append(((zu + 1.0) * (zg * jax.nn.sigmoid(SWIGLU_ALPHA * zg))).astype(BF16))
    act = jnp.concatenate(parts, axis=1)
    y = jnp.dot(act, wd_ref[0].astype(BF16), preferred_element_type=F32) + bd_ref[0]
    y = y.astype(o_ref.dtype)
    row = lax.broadcasted_iota(jnp.int32, (tm, 1), 0)
    mine = (row >= lo_ref[w]) & (row < hi_ref[w])

    @pl.when(first_ref[w] == 1)
    def _():
        o_ref[...] = jnp.where(mine, y, jnp.zeros_like(y))

    @pl.when(first_ref[w] == 0)
    def _():
        o_ref[...] = jnp.where(mine, y, o_ref[...])


def _experts(hs, items, wgu_bf, bgu, wd_all, bd, tm, ebase):
    p, d = hs.shape
    n2 = wgu_bf.shape[2]
    ne = bgu.shape[0]
    de = n2 // 2
    nw = items[0].shape[0]
    grid_spec = pltpu.PrefetchScalarGridSpec(
        num_scalar_prefetch=5,
        grid=(nw,),
        in_specs=[pl.BlockSpec((tm, d), lambda w, t, e, lo, hi, f: (t[w], 0)),
                  pl.BlockSpec((1, d, n2), lambda w, t, e, lo, hi, f: (ebase + e[w], 0, 0)),
                  pl.BlockSpec((1, 1, n2), lambda w, t, e, lo, hi, f: (e[w], 0, 0)),
                  pl.BlockSpec((1, de, d), lambda w, t, e, lo, hi, f: (ebase + e[w], 0, 0)),
                  pl.BlockSpec((1, 1, d), lambda w, t, e, lo, hi, f: (e[w], 0, 0))],
        out_specs=pl.BlockSpec((tm, d), lambda w, t, e, lo, hi, f: (t[w], 0)),
    )
    return pl.pallas_call(
        _expert_kernel,
        grid_spec=grid_spec,
        out_shape=jax.ShapeDtypeStruct((p, d), BF16),
        compiler_params=_cparams(("arbitrary",),
                                 (2 * tm * d * 4 + 2 * tm * d * 2 + 2 * d * n2 * 2 + 2 * de * d * 4 + de * d * 2
                                  + 2 * tm * n2 * 4 + tm * d * 4) / 2 ** 20 + 6),
        name="moe_experts",
    )(*items, hs, wgu_bf, bgu.reshape(ne, 1, n2), wd_all, bd.reshape(ne, 1, d))


def _expert_items(counts, p, tm, ne):
    ntiles = p // tm
    nw = ntiles + ne - 1
    off = jnp.concatenate([jnp.zeros((1,), jnp.int32), jnp.cumsum(counts)])
    first_tile = off[:-1] // tm
    last_tile = (off[1:] - 1) // tm
    ntile_e = jnp.where(counts > 0, last_tile - first_tile + 1, 0)
    base_incl = jnp.cumsum(ntile_e)
    base = base_incl - ntile_e
    total = base_incl[-1]
    w = jnp.arange(nw, dtype=jnp.int32)
    wc = jnp.minimum(w, total - 1)
    e = jnp.sum((base_incl[None, :] <= wc[:, None]).astype(jnp.int32), axis=1)
    t = first_tile[e] + (wc - base[e])
    lo = jnp.clip(off[e] - t * tm, 0, tm)
    hi = jnp.clip(off[e + 1] - t * tm, 0, tm)
    valid = w < total
    lo = jnp.where(valid, lo, 0)
    hi = jnp.where(valid, hi, 0)
    prev_t = jnp.concatenate([jnp.full((1,), -1, jnp.int32), t[:-1]])
    first = (valid & (t != prev_t)).astype(jnp.int32)
    return (t.astype(jnp.int32), e, lo.astype(jnp.int32), hi.astype(jnp.int32), first), off


def _combine_kernel(c0_ref, cc_ref, sb_ref, tot_ref, ys_ref, aux_ref, tab_ref, x_ref, mod_ref, fg_ref, o_ref,
                    buf, sem, *, final, ne, ntiles):
    i = pl.program_id(0)
    tm = x_ref.shape[0]
    nrow = buf.shape[1]
    slot = i % 2

    def fetch(tile, sl):
        def per_expert(e, carry):
            j = tile * ne + e

            def per_chunk(k, c):
                src_row = pl.multiple_of((c0_ref[j] + k) * CHUNK_ROWS, CHUNK_ROWS)
                dst_row = pl.multiple_of((sb_ref[j] + k) * CHUNK_ROWS, CHUNK_ROWS)
                pltpu.make_async_copy(ys_ref.at[pl.ds(src_row, CHUNK_ROWS), :],
                                      buf.at[sl, pl.ds(dst_row, CHUNK_ROWS), :], sem.at[sl]).start()
                return c
            return lax.fori_loop(0, cc_ref[j], per_chunk, carry)
        lax.fori_loop(0, ne, per_expert, 0)

    def wait_fetched(tile, sl):
        total = tot_ref[tile]
        for b in range((nrow // CHUNK_ROWS).bit_length()):
            rows = CHUNK_ROWS << b

            @pl.when((total >> b) & 1 == 1)
            def _():
                pltpu.make_async_copy(ys_ref.at[pl.ds(0, rows), :], buf.at[sl, pl.ds(0, rows), :],
                                      sem.at[sl]).wait()

    @pl.when(i == 0)
    def _():
        buf[...] = jnp.zeros_like(buf)
        fetch(0, 0)

    @pl.when(i + 1 < ntiles)
    def _():
        fetch(i + 1, 1 - slot)

    wait_fetched(i, slot)

    aux = aux_ref[...]
    tab = tab_ref[0]
    lane = lax.broadcasted_iota(jnp.int32, (tm, LANES), 1).astype(F32)
    col = lax.broadcasted_iota(jnp.int32, (tm, nrow), 1).astype(F32)
    gmat = jnp.zeros((tm, nrow), F32)
    for r in range(TOP_K):
        base = jnp.sum(jnp.where(lane == aux[:, r:r + 1], tab, 0.0), axis=-1, keepdims=True)
        lrow = base + aux[:, TOP_K + r:TOP_K + r + 1]
        gmat = gmat + jnp.where(col == lrow, aux[:, 2 * TOP_K + r:2 * TOP_K + r + 1], 0.0)
    y = jnp.dot(gmat.astype(BF16), buf[slot], preferred_element_type=F32)
    xn = x_ref[...] + mod_ref[0][5:6] * y
    if final:
        ms = jnp.mean(xn * xn, axis=-1, keepdims=True)
        xn = xn * lax.rsqrt(ms + EPS) * fg_ref[...]
    o_ref[...] = xn


def _combine_plan(off, before, counts, ne):
    start = off[None, :ne] + before
    end = off[None, :ne] + jnp.concatenate([before[1:], counts[None, :]], axis=0)
    c0 = start // CHUNK_ROWS
    cc = jnp.where(end > start, (end + CHUNK_ROWS - 1) // CHUNK_ROWS - c0, 0)
    sb = jnp.cumsum(cc, axis=1) - cc
    tab = (sb - c0) * CHUNK_ROWS + off[None, :ne]
    tab = jnp.zeros((before.shape[0], 1, LANES), F32).at[:, 0, :ne].set(tab.astype(F32))
    flat = lambda a: a.reshape(-1).astype(jnp.int32)
    return flat(c0), flat(cc), flat(sb), flat(jnp.sum(cc, axis=1)), tab


def _combine(ys, plan, aux, xs, mod_l, final_g, nb, n_lat, nrows, final, ne):
    d = xs.shape[1]
    tm = 256
    bpb = n_lat // tm
    ntiles = nrows // tm
    c0, cc, sb, tot, tab = plan
    max_chunks = tm * TOP_K // CHUNK_ROWS + 2 * ne
    out_rows = nrows if final else xs.shape[0]
    grid_spec = pltpu.PrefetchScalarGridSpec(
        num_scalar_prefetch=4,
        grid=(ntiles,),
        in_specs=[pl.BlockSpec(memory_space=pl.ANY),
                  pl.BlockSpec((tm, LANES), lambda i, *_: (i, 0)),
                  pl.BlockSpec((1, 1, LANES), lambda i, *_: (i, 0, 0)),
                  pl.BlockSpec((tm, d), lambda i, *_: (i, 0)),
                  pl.BlockSpec((1, 6, d), lambda i, *_: (_group_of(i, bpb, nb), 0, 0)),
                  pl.BlockSpec((1, d), lambda i, *_: (0, 0))],
        out_specs=pl.BlockSpec((tm, d), lambda i, *_: (i, 0)),
        scratch_shapes=[pltpu.VMEM((2, max_chunks * CHUNK_ROWS, d), ys.dtype), pltpu.SemaphoreType.DMA((2,))],
    )
    return pl.pallas_call(
        functools.partial(_combine_kernel, final=final, ne=ne, ntiles=ntiles),
        grid_spec=grid_spec,
        out_shape=jax.ShapeDtypeStruct((out_rows, d), F32),
        input_output_aliases={} if final else {7: 0},
        compiler_params=_cparams(("arbitrary",),
                                 (2 * max_chunks * CHUNK_ROWS * d * 2 + 4 * tm * max_chunks * CHUNK_ROWS * 4 + 8 * tm * d * 4) / 2 ** 20 + 6),
        name="moe_combine",
    )(c0, cc, sb, tot, ys, aux, tab, xs, mod_l, final_g.reshape(1, d))


def _moe(xs, mod_l, g, rw, rb, wgu_bf, bgu, wd_all, bd, ebase, final_g, nb, n_lat, nrows, final):
    d = xs.shape[1]
    ne = rw.shape[1]
    rw_pad = jnp.zeros((d, LANES), F32).at[:, :ne].set(rw.astype(F32))
    whi = rw_pad.astype(BF16)
    wlo = (rw_pad - whi.astype(F32)).astype(BF16)
    rb_pad = jnp.full((1, LANES), -1e30, F32).at[0, :ne].set(rb.astype(F32))
    h, aux, cnt, before = _router(xs, mod_l, g, whi, wlo, rb_pad, nb, n_lat, nrows)

    counts = cnt[0, :ne].astype(jnp.int32)
    before = before[:, 0, :ne].astype(jnp.int32)
    tm_e = 512
    p = nrows * TOP_K
    items, off = _expert_items(counts, p, tm_e, ne)
    idx = aux[:, 0:TOP_K].astype(jnp.int32)
    rank = aux[:, TOP_K:2 * TOP_K].astype(jnp.int32)
    pos = (off[idx] + rank).reshape(-1)

    hs = _dispatch(h, pos)
    n2 = bgu.shape[1]
    bgu_p = bgu.astype(F32).reshape(ne, n2 // (2 * LANES), LANES, 2).transpose(0, 1, 3, 2).reshape(ne, n2)
    ys = _experts(hs, items, wgu_bf, bgu_p, wd_all, bd.astype(F32), tm_e, ebase)
    plan = _combine_plan(off, before, counts, ne)
    return _combine(ys, plan, aux, xs, mod_l, final_g, nb, n_lat, nrows, final, ne)


def _rope_tables(nb, n_lat, n_ctx):
    def angles(seq_pos, row_pos, col_pos):
        parts = []
        for posv, dim in zip((seq_pos, row_pos, col_pos), ROPE_AXES):
            inv = ROPE_THETA ** (-jnp.arange(0, dim, 2, dtype=F32) / dim)
            parts.append(posv.astype(F32)[:, None] * inv[None, :])
        return jnp.concatenate(parts, axis=-1)

    rows = n_lat // GRID_W
    lat = angles(jnp.full((n_lat,), n_ctx, jnp.int32),
                 jnp.repeat(jnp.arange(rows, dtype=jnp.int32), GRID_W),
                 jnp.tile(jnp.arange(GRID_W, dtype=jnp.int32), rows))
    zl = jnp.zeros((n_ctx,), jnp.int32)
    ctx = angles(jnp.arange(n_ctx, dtype=jnp.int32), zl, zl)
    ang = jnp.concatenate([lat, jnp.tile(ctx, (nb, 1))], axis=0)
    cos, sin = jnp.cos(ang), jnp.sin(ang)
    return jnp.concatenate([cos, cos], axis=-1), jnp.concatenate([-sin, sin], axis=-1)


def _deinterleave_heads(w):
    d, n = w.shape
    w4 = w.reshape(d, n // RET_DK, RET_DK // 2, 2)
    return jnp.concatenate([w4[..., 0], w4[..., 1]], axis=-1).reshape(d, n)


def kernel(x, c, ctx, c_ctx, mod_w, mod_b, norm_mix_g, norm_ffn_g, ev_w_in, ev_w_out, ret_decay_f, ret_decay_b,
           ret_gn_g, conv_w, conv_b, conv_ln_g, conv_ln_b, od_w_in, od_w_out, sgu_ln_g, sgu_ln_b, sgu_w, sgu_b,
           router_w, router_b, moe_w_gu, moe_b_gu, moe_w_down, moe_b_down, final_g):
    nb, n_lat, d = x.shape
    n_ctx = ctx.shape[1]
    depth = mod_w.shape[0]
    n_lat_total = nb * n_lat
    n_ctx_total = nb * n_ctx
    nt = n_lat_total + n_ctx_total
    assert nb < MOD_GROUPS and n_lat % GRID_W == 0
    qk = RET_HEADS * RET_DK

    xs = jnp.concatenate([x.reshape(n_lat_total, d), ctx.reshape(n_ctx_total, d)], axis=0).astype(F32)
    cc = jnp.zeros((MOD_GROUPS, d), F32).at[:nb].set(c.astype(F32)).at[nb].set(c_ctx.astype(F32))
    mod = _mod_table(cc, mod_w, mod_b).reshape(depth, MOD_GROUPS, 6, d)
    cos_t, sin_t = _rope_tables(nb, n_lat, n_ctx)
    ne = router_w.shape[2]
    wgu_bf = _prep_wgu(moe_w_gu.reshape((depth * ne,) + moe_w_gu.shape[2:]))
    wd_all = moe_w_down.reshape((depth * ne,) + moe_w_down.shape[2:])

    out = None
    for layer in range(depth):
        last = layer == depth - 1
        nrows = n_lat_total if last else nt
        mod_l = mod[layer]
        if layer % 2 == 0:
            e = layer // 2
            w_in = ev_w_in[e]
            w_in = jnp.concatenate([_deinterleave_heads(w_in[:, :qk]), _deinterleave_heads(w_in[:, qk:2 * qk]),
                                    w_in[:, 2 * qk:]], axis=1).astype(BF16)
            z = _premix_even(xs, mod_l, norm_mix_g[layer], w_in, cos_t, sin_t, nb, n_lat, n_ctx_total)
            lg = jnp.stack([-jnp.exp(ret_decay_f[e].astype(F32)), -jnp.exp(ret_decay_b[e].astype(F32))])
            gn = ret_gn_g[e].reshape(1, qk).astype(F32)
            s0 = jnp.zeros((nb, RET_HEADS, 2, RET_DK, RET_DK), F32)
            ret0 = jnp.zeros((nt, qk), BF16)
            ret, st = _retention(z, lg, gn, s0, ret0, nb, n_ctx, n_lat_total)
            ret, _ = _retention(z, lg, gn, st, ret, nb, n_lat, 0)
            cv = _conv_module(z, conv_w[e].astype(F32), conv_b[e].astype(F32), conv_ln_g[e].astype(F32),
                              conv_ln_b[e].astype(F32), nb, n_lat, n_ctx)
            xs = _outproj_even(ret, cv, ev_w_out[e].astype(BF16), xs, mod_l, nb, n_lat, n_ctx_total)
        else:
            j = layer // 2
            uv = _premix_odd(xs, mod_l, norm_mix_g[layer], od_w_in[j].astype(BF16), nb, n_lat, nrows)
            sw = od_w_out.shape[1]
            bias_full = jnp.repeat(sgu_b[j].astype(F32).T, sw // SGU_GROUPS, axis=1)
            xs = _sgu_out(uv, sgu_ln_g[j].astype(F32), sgu_ln_b[j].astype(F32), sgu_w[j].astype(BF16), bias_full,
                          od_w_out[j].astype(BF16), xs, mod_l, nb, n_lat, nrows)
        res = _moe(xs, mod_l, norm_ffn_g[layer], router_w[layer], router_b[layer], wgu_bf, moe_b_gu[layer],
                   wd_all, moe_b_down[layer], layer * ne, final_g.astype(F32), nb, n_lat, nrows, last)
        if last:
            out = res
        else:
            xs = res
    return out.reshape(nb, n_lat, d).astype(x.dtype)
```

```python
import functools
import math

import jax
import jax.numpy as jnp
from jax import lax
from jax.experimental import pallas as pl
from jax.experimental.pallas import tpu as pltpu

F32 = jnp.float32
BF16 = jnp.bfloat16

LANES = 128
CHUNK_ROWS = 16
RET_HEADS = 8
RET_DK = 128
RET_CHUNK = 128
ROPE_AXES = (32, 48, 48)
ROPE_THETA = 10000.0
GRID_W = 64
SGU_GROUPS = 8
SGU_CHUNK = 128
TOP_K = 4
MOE_TILE = 256
FETCH_UNROLL = 4
SWIGLU_LIMIT = 7.0
SWIGLU_ALPHA = 1.702
EPS = 1e-6
MOD_GROUPS = 8
HALO = 16
VMEM_CAP = 58 * 2 ** 20


def _cparams(sem, vmem_mb):
    return pltpu.CompilerParams(dimension_semantics=sem,
                                vmem_limit_bytes=min(int(vmem_mb * 2 ** 20), VMEM_CAP))


def _pick_tile(n_lat, n_ctx, cands):
    for t in cands:
        if n_lat % t == 0 and n_ctx % t == 0:
            return t
    raise ValueError("no token tile divides the latent and context lengths")


def _group_of(i, blocks_per_batch, nb):
    return jnp.minimum(i // blocks_per_batch, nb)


def _norm_mod(x, g, shift, scale):
    ms = jnp.mean(x * x, axis=-1, keepdims=True)
    return (x * lax.rsqrt(ms + EPS) * g) * (1.0 + scale) + shift


def _mod_kernel(c_ref, w_ref, b_ref, o_ref):
    c = c_ref[...]
    s = (c * jax.nn.sigmoid(c)).astype(BF16)
    o_ref[0] = jnp.dot(s, w_ref[0].astype(BF16), preferred_element_type=F32) + b_ref[0]


def _mod_table(cc, mod_w, mod_b):
    depth, d, n6 = mod_w.shape
    tn = min(d, 1024)
    return pl.pallas_call(
        _mod_kernel,
        grid=(depth, n6 // tn),
        in_specs=[pl.BlockSpec((MOD_GROUPS, d), lambda l, j: (0, 0)),
                  pl.BlockSpec((1, d, tn), lambda l, j: (l, 0, j)),
                  pl.BlockSpec((1, 1, tn), lambda l, j: (l, 0, j))],
        out_specs=pl.BlockSpec((1, MOD_GROUPS, tn), lambda l, j: (l, 0, j)),
        out_shape=jax.ShapeDtypeStruct((depth, MOD_GROUPS, n6), F32),
        compiler_params=_cparams(("arbitrary", "arbitrary"), 2 * d * tn * 4 / 2 ** 20 + 8),
        name="mod_table",
    )(cc, mod_w, mod_b.reshape(depth, 1, n6))


def _premix_even_kernel(x_ref, mod_ref, g_ref, w_ref, cos_ref, sin_ref, o_ref, h_scr, *, kscale):
    j = pl.program_id(1)

    @pl.when(j == 0)
    def _():
        m = mod_ref[0]
        h_scr[...] = _norm_mod(x_ref[...], g_ref[...], m[0:1], m[1:2]).astype(BF16)

    z = jnp.dot(h_scr[...], w_ref[...], preferred_element_type=F32)

    @pl.when(j < 2)
    def _():
        c = cos_ref[...]
        s = sin_ref[...]
        scale = jnp.where(j == 1, kscale, 1.0).astype(F32)
        for hh in range(z.shape[1] // RET_DK):
            t = z[:, hh * RET_DK:(hh + 1) * RET_DK]
            r = pltpu.roll(t, RET_DK // 2, axis=1)
            o_ref[:, hh * RET_DK:(hh + 1) * RET_DK] = ((t * c + r * s) * scale).astype(BF16)

    @pl.when(j >= 2)
    def _():
        o_ref[...] = z.astype(BF16)


def _premix_even(xs, mod_l, g, w_bf, cos_t, sin_t, nb, n_lat, n_ctx_total):
    nt, d = xs.shape
    n_out = w_bf.shape[1]
    tm = _pick_tile(n_lat, n_ctx_total, (1024, 512, 256))
    tn = RET_HEADS * RET_DK
    bpb = n_lat // tm
    n_lat_blocks = nb * bpb

    def tab_idx(i, j):
        return (jnp.where(i < n_lat_blocks, i % bpb, bpb + (i - n_lat_blocks)), 0)

    return pl.pallas_call(
        functools.partial(_premix_even_kernel, kscale=RET_DK ** -0.5),
        grid=(nt // tm, n_out // tn),
        in_specs=[pl.BlockSpec((tm, d), lambda i, j: (i, 0)),
                  pl.BlockSpec((1, 6, d), lambda i, j: (_group_of(i, bpb, nb), 0, 0)),
                  pl.BlockSpec((1, d), lambda i, j: (0, 0)),
                  pl.BlockSpec((d, tn), lambda i, j: (0, j)),
                  pl.BlockSpec((tm, RET_DK), tab_idx),
                  pl.BlockSpec((tm, RET_DK), tab_idx)],
        out_specs=pl.BlockSpec((tm, tn), lambda i, j: (i, j)),
        out_shape=jax.ShapeDtypeStruct((nt, n_out), BF16),
        scratch_shapes=[pltpu.VMEM((tm, d), BF16)],
        compiler_params=_cparams(("arbitrary", "arbitrary"),
                                 (2 * tm * d * 4 + tm * d * 2 + 2 * d * tn * 2 + 2 * tm * tn * 2
                                  + 2 * tm * tn * 4) / 2 ** 20 + 8),
        name="premix_even",
    )(xs, mod_l, g.reshape(1, d), w_bf, cos_t, sin_t)


def _retention_kernel(lg_ref, q_ref, k_ref, v_ref, g_ref, gn_ref, s0_ref, ret_in_ref, ret_ref, sfin_ref,
                      st_scr, *, nc, unroll):
    del ret_in_ref
    hh = pl.program_id(1)
    lgf = lg_ref[0, hh]
    lgb = lg_ref[1, hh]
    c = RET_CHUNK
    ri = lax.broadcasted_iota(jnp.int32, (c, c), 0).astype(F32)
    ci = lax.broadcasted_iota(jnp.int32, (c, c), 1).astype(F32)
    rel = ri - ci
    dmat = jnp.where(rel >= 0, jnp.exp(lgf * jnp.maximum(rel, 0.0)), jnp.exp(lgb * jnp.maximum(-rel, 0.0)))
    xi_f = jnp.exp(lgf * (ri + 1.0))
    xi_b = jnp.exp(lgb * (c - ri))
    zeta_f = jnp.exp(lgf * (c - 1.0 - ci))
    zeta_b = jnp.exp(lgb * ci)
    gc_f = jnp.exp(lgf * c + jnp.zeros((c, c), F32))
    gc_b = jnp.exp(lgb * c + jnp.zeros((c, c), F32))

    def local_sums(i, carry):
        for u in range(unroll):
            cc = i * unroll + u
            r0 = pl.multiple_of(cc * c, c)
            kt = k_ref[pl.ds(r0, c), :].astype(F32).T
            v = v_ref[pl.ds(r0, c), :]
            st_scr[cc, :, 0:c] = jnp.dot((kt * zeta_f).astype(BF16), v, preferred_element_type=F32)
            st_scr[cc, :, c:2 * c] = jnp.dot((kt * zeta_b).astype(BF16), v, preferred_element_type=F32)
        return carry

    lax.fori_loop(0, nc // unroll, local_sums, 0)

    def scans(i, carry):
        sf, sb = carry
        cb = nc - 1 - i
        uf = st_scr[i, :, 0:c]
        st_scr[i, :, 0:c] = sf
        ub = st_scr[cb, :, c:2 * c]
        st_scr[cb, :, c:2 * c] = sb
        return gc_f * sf + uf, gc_b * sb + ub

    sf, sb = lax.fori_loop(0, nc, scans, (s0_ref[0, 0, 0], s0_ref[0, 0, 1]))
    sfin_ref[0, 0, 0] = sf
    sfin_ref[0, 0, 1] = sb

    def outputs(i, carry):
        for u in range(unroll):
            cc = i * unroll + u
            r0 = pl.multiple_of(cc * c, c)
            q = q_ref[pl.ds(r0, c), :]
            a = lax.dot_general(q, k_ref[pl.ds(r0, c), :], (((1,), (1,)), ((), ())), preferred_element_type=F32)
            o = jnp.dot((a * dmat).astype(BF16), v_ref[pl.ds(r0, c), :], preferred_element_type=F32)
            cr = jnp.dot(q, st_scr[cc].astype(BF16), preferred_element_type=F32)
            o = o + cr[:, 0:c] * xi_f + cr[:, c:2 * c] * xi_b
            mu = jnp.mean(o, axis=-1, keepdims=True)
            dlt = o - mu
            var = jnp.mean(dlt * dlt, axis=-1, keepdims=True)
            y = dlt * lax.rsqrt(var + EPS) * gn_ref[...]
            g = g_ref[pl.ds(r0, c), :].astype(F32)
            ret_ref[pl.ds(r0, c), :] = (g * jax.nn.sigmoid(g) * y).astype(BF16)
        return carry

    lax.fori_loop(0, nc // unroll, outputs, 0)


def _retention(z, lg, gn, s0, ret_prev, nb, seq, row0):
    nt = z.shape[0]
    nc = seq // RET_CHUNK
    rb0 = row0 // seq
    hcols = RET_HEADS

    def col(off):
        return lambda b, h: (rb0 + b, off * hcols + h)

    ret, sfin = pl.pallas_call(
        functools.partial(_retention_kernel, nc=nc, unroll=max(u for u in (8, 4, 2, 1) if nc % u == 0)),
        grid=(nb, RET_HEADS),
        in_specs=[pl.BlockSpec(memory_space=pltpu.SMEM),
                  pl.BlockSpec((seq, RET_DK), col(0)),
                  pl.BlockSpec((seq, RET_DK), col(1)),
                  pl.BlockSpec((seq, RET_DK), col(2)),
                  pl.BlockSpec((seq, RET_DK), col(3)),
                  pl.BlockSpec((1, RET_DK), lambda b, h: (0, h)),
                  pl.BlockSpec((1, 1, 2, RET_DK, RET_DK), lambda b, h: (b, h, 0, 0, 0)),
                  pl.BlockSpec(memory_space=pl.ANY)],
        out_specs=[pl.BlockSpec((seq, RET_DK), lambda b, h: (rb0 + b, h)),
                   pl.BlockSpec((1, 1, 2, RET_DK, RET_DK), lambda b, h: (b, h, 0, 0, 0))],
        out_shape=[jax.ShapeDtypeStruct((nt, RET_HEADS * RET_DK), BF16),
                   jax.ShapeDtypeStruct((nb, RET_HEADS, 2, RET_DK, RET_DK), F32)],
        scratch_shapes=[pltpu.VMEM((nc, RET_DK, 2 * RET_DK), F32)],
        input_output_aliases={7: 0},
        compiler_params=_cparams(("arbitrary", "arbitrary"),
                                 (10 * seq * RET_DK * 2 + 2 * nc * RET_DK * RET_DK * 4) / 2 ** 20 + 8),
        name="retention",
    )(lg, z, z, z, z, gn, s0, ret_prev)
    return ret, sfin


def _conv_kernel(a_ref, b_ref, ap_ref, bp_ref, an_ref, bn_ref, w_ref, cb_ref, lg_ref, lb_ref, o_ref, ext, shifted,
                 *, tm, n_lat_total, n_lat, n_ctx, ktaps):
    i = pl.program_id(0)
    row0 = i * tm
    is_lat = row0 < n_lat_total
    pos = jnp.where(is_lat, row0 % n_lat, (row0 - n_lat_total) % n_ctx)
    seq = jnp.where(is_lat, n_lat, n_ctx)
    keep_prev = (pos != 0).astype(F32)
    keep_next = (pos + tm != seq).astype(F32)

    def glu(a, b):
        return a.astype(F32) * jax.nn.sigmoid(b.astype(F32))

    ext[0:HALO, :] = glu(ap_ref[...], bp_ref[...]) * keep_prev
    ext[HALO:HALO + tm, :] = glu(a_ref[...], b_ref[...])
    ext[HALO + tm:HALO + tm + HALO, :] = glu(an_ref[...], bn_ref[...]) * keep_next

    half = ktaps // 2
    rc = 32
    sub = 8
    span = shifted.shape[1]
    for s in range(1, sub):
        shifted[s - 1] = ext[s:s + span, :]
    for r in range(tm // rc):
        acc = jnp.zeros((rc, a_ref.shape[1]), F32)
        for j in range(ktaps):
            o = HALO + r * rc + j - half
            s, base = o % sub, o - o % sub
            win = ext[base:base + rc, :] if s == 0 else shifted[s - 1, base:base + rc, :]
            acc = acc + w_ref[j:j + 1, :] * win
        y = acc + cb_ref[...]
        mu = jnp.mean(y, axis=-1, keepdims=True)
        dlt = y - mu
        var = jnp.mean(dlt * dlt, axis=-1, keepdims=True)
        yn = dlt * lax.rsqrt(var + EPS) * lg_ref[...] + lb_ref[...]
        o_ref[r * rc:(r + 1) * rc, :] = (yn * jax.nn.sigmoid(yn)).astype(BF16)


def _conv_module(z, conv_w, conv_b, ln_g, ln_b, nb, n_lat, n_ctx):
    nt = z.shape[0]
    ktaps, cw = conv_w.shape
    tm = 256
    assert n_lat % tm == 0 and n_ctx % tm == 0 and ktaps // 2 <= HALO
    a_blk = (2 * RET_HEADS * RET_DK + 2 * RET_HEADS * RET_DK) // cw
    hb = tm // HALO
    last = nt // HALO - 1
    kern = functools.partial(_conv_kernel, tm=tm, n_lat_total=nb * n_lat, n_lat=n_lat, n_ctx=n_ctx, ktaps=ktaps)
    vec = lambda i: (0, 0)
    return pl.pallas_call(
        kern,
        grid=(nt // tm,),
        in_specs=[pl.BlockSpec((tm, cw), lambda i: (i, a_blk)),
                  pl.BlockSpec((tm, cw), lambda i: (i, a_blk + 1)),
                  pl.BlockSpec((HALO, cw), lambda i: (jnp.maximum(i * hb - 1, 0), a_blk)),
                  pl.BlockSpec((HALO, cw), lambda i: (jnp.maximum(i * hb - 1, 0), a_blk + 1)),
                  pl.BlockSpec((HALO, cw), lambda i: (jnp.minimum((i + 1) * hb, last), a_blk)),
                  pl.BlockSpec((HALO, cw), lambda i: (jnp.minimum((i + 1) * hb, last), a_blk + 1)),
                  pl.BlockSpec((ktaps, cw), vec),
                  pl.BlockSpec((1, cw), vec),
                  pl.BlockSpec((1, cw), vec),
                  pl.BlockSpec((1, cw), vec)],
        out_specs=pl.BlockSpec((tm, cw), lambda i: (i, 0)),
        out_shape=jax.ShapeDtypeStruct((nt, cw), BF16),
        scratch_shapes=[pltpu.VMEM((tm + 2 * HALO, cw), F32),
                        pltpu.VMEM((7, tm + 2 * HALO - 8, cw), F32)],
        compiler_params=_cparams(("arbitrary",), 40),
        name="conv_module",
    )(z, z, z, z, z, z, conv_w, conv_b.reshape(1, cw), ln_g.reshape(1, cw), ln_b.reshape(1, cw))


def _outproj_even_kernel(ret_ref, cv_ref, w_ref, x_ref, mod_ref, o_ref):
    rw = ret_ref.shape[1]
    y = jnp.dot(ret_ref[...], w_ref[0:rw, :], preferred_element_type=F32)
    y = y + jnp.dot(cv_ref[...], w_ref[rw:, :], preferred_element_type=F32)
    o_ref[...] = x_ref[...] + mod_ref[0][2:3] * y


def _outproj_even(ret, cv, w_bf, xs, mod_l, nb, n_lat, n_ctx_total):
    nt, d = xs.shape
    tm = _pick_tile(n_lat, n_ctx_total, (512, 256))
    bpb = n_lat // tm
    rw, cw = ret.shape[1], cv.shape[1]
    return pl.pallas_call(
        _outproj_even_kernel,
        grid=(nt // tm,),
        in_specs=[pl.BlockSpec((tm, rw), lambda i: (i, 0)),
                  pl.BlockSpec((tm, cw), lambda i: (i, 0)),
                  pl.BlockSpec((rw + cw, d), lambda i: (0, 0)),
                  pl.BlockSpec((tm, d), lambda i: (i, 0)),
                  pl.BlockSpec((1, 6, d), lambda i: (_group_of(i, bpb, nb), 0, 0))],
        out_specs=pl.BlockSpec((tm, d), lambda i: (i, 0)),
        out_shape=jax.ShapeDtypeStruct((nt, d), F32),
        input_output_aliases={3: 0},
        compiler_params=_cparams(("arbitrary",),
                                 (2 * (rw + cw) * d * 2 + 4 * tm * d * 4 + 2 * tm * (rw + cw) * 2
                                  + tm * d * 4) / 2 ** 20 + 8),
        name="outproj_even",
    )(ret, cv, w_bf, xs, mod_l)


def _premix_odd_kernel(x_ref, mod_ref, g_ref, w_ref, o_ref, h_scr):
    j = pl.program_id(1)

    @pl.when(j == 0)
    def _():
        m = mod_ref[0]
        h_scr[...] = _norm_mod(x_ref[...], g_ref[...], m[0:1], m[1:2]).astype(BF16)

    z = jnp.dot(h_scr[...], w_ref[...], preferred_element_type=F32)
    o_ref[...] = (0.5 * z * (1.0 + lax.erf(z * (2.0 ** -0.5)))).astype(BF16)


def _premix_odd(xs, mod_l, g, w_bf, nb, n_lat, nrows):
    d = xs.shape[1]
    n_out = w_bf.shape[1]
    tm = _pick_tile(n_lat, nrows, (1024, 512, 256))
    tn = 1024
    bpb = n_lat // tm
    return pl.pallas_call(
        _premix_odd_kernel,
        grid=(nrows // tm, n_out // tn),
        in_specs=[pl.BlockSpec((tm, d), lambda i, j: (i, 0)),
                  pl.BlockSpec((1, 6, d), lambda i, j: (_group_of(i, bpb, nb), 0, 0)),
                  pl.BlockSpec((1, d), lambda i, j: (0, 0)),
                  pl.BlockSpec((d, tn), lambda i, j: (0, j))],
        out_specs=pl.BlockSpec((tm, tn), lambda i, j: (i, j)),
        out_shape=jax.ShapeDtypeStruct((nrows, n_out), BF16),
        scratch_shapes=[pltpu.VMEM((tm, d), BF16)],
        compiler_params=_cparams(("arbitrary", "arbitrary"),
                                 (2 * tm * d * 4 + tm * d * 2 + 2 * d * tn * 2 + 2 * tm * tn * 2
                                  + 2 * tm * tn * 4) / 2 ** 20 + 8),
        name="premix_odd",
    )(xs, mod_l, g.reshape(1, d), w_bf)


def _sgu_out_kernel(u_ref, v_ref, lg_ref, lb_ref, ws_ref, bs_ref, w_ref, x_ref, mod_ref, o_ref, p_scr):
    tm, sw = u_ref.shape
    gw = sw // SGU_GROUPS
    v = v_ref[...].astype(F32)
    mu = jnp.mean(v, axis=-1, keepdims=True)
    dlt = v - mu
    var = jnp.mean(dlt * dlt, axis=-1, keepdims=True)
    p_scr[...] = (dlt * lax.rsqrt(var + EPS) * lg_ref[...] + lb_ref[...]).astype(BF16)
    for cc in range(tm // SGU_CHUNK):
        rows = slice(cc * SGU_CHUNK, (cc + 1) * SGU_CHUNK)
        for gi in range(SGU_GROUPS):
            cols = slice(gi * gw, (gi + 1) * gw)
            mixed = jnp.dot(ws_ref[gi], p_scr[rows, cols], preferred_element_type=F32) + bs_ref[:, cols]
            p_scr[rows, cols] = (u_ref[rows, cols].astype(F32) * mixed).astype(BF16)
    y = jnp.dot(p_scr[...], w_ref[...], preferred_element_type=F32)
    o_ref[...] = x_ref[...] + mod_ref[0][2:3] * y


def _sgu_out(uv, ln_g, ln_b, ws_bf, bias_full, w_bf, xs, mod_l, nb, n_lat, nrows):
    nt, d = xs.shape
    sw = w_bf.shape[0]
    tm = _pick_tile(n_lat, nrows, (512, 256))
    bpb = n_lat // tm
    vec = lambda i: (0, 0)
    return pl.pallas_call(
        _sgu_out_kernel,
        grid=(nrows // tm,),
        in_specs=[pl.BlockSpec((tm, sw), lambda i: (i, 0)),
                  pl.BlockSpec((tm, sw), lambda i: (i, 1)),
                  pl.BlockSpec((1, sw), vec),
                  pl.BlockSpec((1, sw), vec),
                  pl.BlockSpec((SGU_GROUPS, SGU_CHUNK, SGU_CHUNK), lambda i: (0, 0, 0)),
                  pl.BlockSpec((SGU_CHUNK, sw), vec),
                  pl.BlockSpec((sw, d), vec),
                  pl.BlockSpec((tm, d), lambda i: (i, 0)),
                  pl.BlockSpec((1, 6, d), lambda i: (_group_of(i, bpb, nb), 0, 0))],
        out_specs=pl.BlockSpec((tm, d), lambda i: (i, 0)),
        out_shape=jax.ShapeDtypeStruct((nt, d), F32),
        scratch_shapes=[pltpu.VMEM((tm, sw), BF16)],
        input_output_aliases={7: 0},
        compiler_params=_cparams(("arbitrary",),
                                 (2 * sw * d * 2 + 4 * tm * d * 4 + 4 * tm * sw * 2 + tm * sw * 2
                                  + 2 * SGU_CHUNK * sw * 4 + 2 * tm * sw * 4) / 2 ** 20 + 8),
        name="sgu_out",
    )(uv, uv, ln_g.reshape(1, sw), ln_b.reshape(1, sw), ws_bf, bias_full, w_bf, xs, mod_l)


def _router_kernel(x_ref, mod_ref, g_ref, whi_ref, wlo_ref, rb_ref, h_ref, aux_ref, cnt_ref, before_ref, carry):
    i = pl.program_id(0)
    tm = x_ref.shape[0]

    @pl.when(i == 0)
    def _():
        carry[...] = jnp.zeros_like(carry)

    m = mod_ref[0]
    h = _norm_mod(x_ref[...], g_ref[...], m[3:4], m[4:5])
    h_ref[...] = h
    h_hi = h.astype(BF16)
    h_lo = (h - h_hi.astype(F32)).astype(BF16)
    logits = (jnp.dot(h_hi, whi_ref[...], preferred_element_type=F32)
              + jnp.dot(h_lo, whi_ref[...], preferred_element_type=F32)
              + jnp.dot(h_hi, wlo_ref[...], preferred_element_type=F32)) + rb_ref[...]

    lane = lax.broadcasted_iota(jnp.int32, (tm, LANES), 1).astype(F32)
    vals, idxs = [], []
    l = logits
    for _ in range(TOP_K):
        mx = jnp.max(l, axis=-1, keepdims=True)
        ix = jnp.min(jnp.where(l == mx, lane, float(LANES)), axis=-1, keepdims=True)
        vals.append(mx)
        idxs.append(ix)
        l = jnp.where(lane == ix, -jnp.inf, l)
    es = [jnp.exp(v - vals[0]) for v in vals]
    den = es[0]
    for e in es[1:]:
        den = den + e

    sel = jnp.zeros((tm, LANES), F32)
    for ix in idxs:
        sel = sel + jnp.where(lane == ix, 1.0, 0.0)
    rr = lax.broadcasted_iota(jnp.int32, (tm, tm), 0)
    rc = lax.broadcasted_iota(jnp.int32, (tm, tm), 1)
    ltri = jnp.where(rc < rr, 1.0, 0.0).astype(BF16)
    rank = jnp.dot(ltri, sel.astype(BF16), preferred_element_type=F32) + carry[...]
    before_ref[0] = jnp.broadcast_to(carry[...], before_ref.shape[1:])
    carry[...] = carry[...] + jnp.sum(sel, axis=0, keepdims=True)
    cnt_ref[...] = carry[...]

    aux = jnp.zeros((tm, LANES), F32)
    for r in range(TOP_K):
        rk = jnp.sum(jnp.where(lane == idxs[r], rank, 0.0), axis=-1, keepdims=True)
        aux = jnp.where(lane == float(r), idxs[r], aux)
        aux = jnp.where(lane == float(TOP_K + r), rk, aux)
        aux = jnp.where(lane == float(2 * TOP_K + r), es[r] / den, aux)
    aux_ref[...] = aux


def _router(xs, mod_l, g, whi, wlo, rb, nb, n_lat, nrows):
    d = xs.shape[1]
    tm = MOE_TILE
    bpb = n_lat // tm
    vec = lambda i: (0, 0)
    return pl.pallas_call(
        _router_kernel,
        grid=(nrows // tm,),
        in_specs=[pl.BlockSpec((tm, d), lambda i: (i, 0)),
                  pl.BlockSpec((1, 6, d), lambda i: (_group_of(i, bpb, nb), 0, 0)),
                  pl.BlockSpec((1, d), vec),
                  pl.BlockSpec((d, LANES), vec),
                  pl.BlockSpec((d, LANES), vec),
                  pl.BlockSpec((1, LANES), vec)],
        out_specs=[pl.BlockSpec((tm, d), lambda i: (i, 0)),
                   pl.BlockSpec((tm, LANES), lambda i: (i, 0)),
                   pl.BlockSpec((1, LANES), vec),
                   pl.BlockSpec((1, 8, LANES), lambda i: (i, 0, 0))],
        out_shape=[jax.ShapeDtypeStruct((nrows, d), F32),
                   jax.ShapeDtypeStruct((nrows, LANES), F32),
                   jax.ShapeDtypeStruct((1, LANES), F32),
                   jax.ShapeDtypeStruct((nrows // tm, 8, LANES), F32)],
        scratch_shapes=[pltpu.VMEM((1, LANES), F32)],
        compiler_params=_cparams(("arbitrary",), 32),
        name="router",
    )(xs, mod_l, g.reshape(1, d), whi, wlo, rb)


def _dispatch_kernel(pos_ref, h_ref, hs_ref, sem):
    tm = h_ref.shape[0]

    def row_copy(t, p):
        return pltpu.make_async_copy(h_ref.at[pl.ds(t, 1), :], hs_ref.at[pl.ds(p, 1), :], sem)

    def issue(t, carry):
        for r in range(TOP_K):
            row_copy(t, pos_ref[0, 0, t * TOP_K + r]).start()
        return carry

    lax.fori_loop(0, tm, issue, 0)
    for _ in range(TOP_K):
        pltpu.make_async_copy(h_ref, hs_ref.at[pl.ds(0, tm), :], sem).wait()


def _dispatch(h, pos):
    nrows, d = h.shape
    tm = MOE_TILE
    pos3 = pos.reshape(nrows // tm, 1, tm * TOP_K)
    return pl.pallas_call(
        _dispatch_kernel,
        grid=(nrows // tm,),
        in_specs=[pl.BlockSpec((1, 1, tm * TOP_K), lambda i: (i, 0, 0), memory_space=pltpu.SMEM),
                  pl.BlockSpec((tm, d), lambda i: (i, 0))],
        out_specs=pl.BlockSpec(memory_space=pl.ANY),
        out_shape=jax.ShapeDtypeStruct((nrows * TOP_K, d), F32),
        scratch_shapes=[pltpu.SemaphoreType.DMA(())],
        compiler_params=_cparams(("arbitrary",), 16),
        name="moe_dispatch",
    )(pos3, h)


def _pair_perm():
    i = jnp.arange(2 * LANES)[:, None]
    j = jnp.arange(2 * LANES)[None, :]
    return jnp.where(j < LANES, i == 2 * j, i == 2 * (j - LANES) + 1).astype(BF16)


def _prep_wgu_kernel(w_ref, p_ref, o_ref):
    blk = 2 * LANES
    for b in range(w_ref.shape[2] // blk):
        cols = slice(b * blk, (b + 1) * blk)
        o_ref[0, :, cols] = jnp.dot(w_ref[0, :, cols].astype(BF16), p_ref[...],
                                    preferred_element_type=F32).astype(BF16)


def _prep_wgu(wgu):
    lead = wgu.shape[:-2]
    d, n2 = wgu.shape[-2:]
    ne = math.prod(lead)
    tr = 1024 if d % 1024 == 0 else d
    out = pl.pallas_call(
        _prep_wgu_kernel,
        grid=(ne, d // tr),
        in_specs=[pl.BlockSpec((1, tr, n2), lambda e, r: (e, r, 0)),
                  pl.BlockSpec((2 * LANES, 2 * LANES), lambda e, r: (0, 0))],
        out_specs=pl.BlockSpec((1, tr, n2), lambda e, r: (e, r, 0)),
        out_shape=jax.ShapeDtypeStruct((ne, d, n2), BF16),
        compiler_params=_cparams(("arbitrary", "arbitrary"), 3 * tr * n2 * 4 / 2 ** 20 + 8),
        name="prep_wgu",
    )(wgu.reshape(ne, d, n2), _pair_perm())
    return out.reshape(lead + (d, n2))


def _expert_kernel(tile_ref, exp_ref, lo_ref, hi_ref, first_ref, hs_ref, wgu_ref, bgu_ref, wd_ref, bd_ref, o_ref):
    del tile_ref, exp_ref
    w = pl.program_id(0)
    tm = hs_ref.shape[0]
    lo, hi, first = lo_ref[w], hi_ref[w], first_ref[w]

    def rows_block(r0, nr):
        rows = slice(r0, r0 + nr)
        z = jnp.dot(hs_ref[rows, :].astype(BF16), wgu_ref[0], preferred_element_type=F32) + bgu_ref[0]
        parts = []
        for b in range(z.shape[1] // (2 * LANES)):
            zg = jnp.minimum(z[:, 2 * b * LANES:(2 * b + 1) * LANES], SWIGLU_LIMIT)
            zu = jnp.clip(z[:, (2 * b + 1) * LANES:(2 * b + 2) * LANES], -SWIGLU_LIMIT, SWIGLU_LIMIT)
            parts.append(((zu + 1.0) * (zg * jax.nn.sigmoid(SWIGLU_ALPHA * zg))).astype(BF16))
        act = jnp.concatenate(parts, axis=1)
        y = jnp.dot(act, wd_ref[0].astype(BF16), preferred_element_type=F32) + bd_ref[0]
        y = y.astype(o_ref.dtype)
        row = r0 + lax.broadcasted_iota(jnp.int32, (nr, 1), 0)
        mine = (row >= lo) & (row < hi)

        @pl.when(first == 1)
        def _():
            o_ref[rows, :] = jnp.where(mine, y, jnp.zeros_like(y))

        @pl.when(first == 0)
        def _():
            o_ref[rows, :] = jnp.where(mine, y, o_ref[rows, :])

    whole = (lo == 0) & (hi == tm)

    @pl.when(whole)
    def _():
        rows_block(0, tm)

    @pl.when(jnp.logical_not(whole))
    def _():
        half = tm // 2
        for r0 in (0, half):
            touched = (lo < r0 + half) & (hi > r0)

            @pl.when(touched)
            def _():
                rows_block(r0, half)

            @pl.when(jnp.logical_not(touched) & (first == 1))
            def _():
                o_ref[r0:r0 + half, :] = jnp.zeros((half, o_ref.shape[1]), o_ref.dtype)


def _experts(hs, items, wgu_bf, bgu, wd_all, bd, tm, ebase):
    p, d = hs.shape
    n2 = wgu_bf.shape[2]
    ne = bgu.shape[0]
    de = n2 // 2
    nw = items[0].shape[0]
    grid_spec = pltpu.PrefetchScalarGridSpec(
        num_scalar_prefetch=5,
        grid=(nw,),
        in_specs=[pl.BlockSpec((tm, d), lambda w, t, e, lo, hi, f: (t[w], 0)),
                  pl.BlockSpec((1, d, n2), lambda w, t, e, lo, hi, f: (ebase + e[w], 0, 0)),
                  pl.BlockSpec((1, 1, n2), lambda w, t, e, lo, hi, f: (e[w], 0, 0)),
                  pl.BlockSpec((1, de, d), lambda w, t, e, lo, hi, f: (ebase + e[w], 0, 0)),
                  pl.BlockSpec((1, 1, d), lambda w, t, e, lo, hi, f: (e[w], 0, 0))],
        out_specs=pl.BlockSpec((tm, d), lambda w, t, e, lo, hi, f: (t[w], 0)),
    )
    return pl.pallas_call(
        _expert_kernel,
        grid_spec=grid_spec,
        out_shape=jax.ShapeDtypeStruct((p, d), BF16),
        compiler_params=_cparams(("arbitrary",),
                                 (2 * tm * d * 4 + 2 * tm * d * 2 + 2 * d * n2 * 2 + 2 * de * d * 4 + de * d * 2
                                  + 2 * tm * n2 * 4 + tm * d * 4) / 2 ** 20 + 6),
        name="moe_experts",
    )(*items, hs, wgu_bf, bgu.reshape(ne, 1, n2), wd_all, bd.reshape(ne, 1, d))


def _expert_items(counts, p, tm, ne):
    ntiles = p // tm
    nw = ntiles + ne - 1
    off = jnp.concatenate([jnp.zeros((1,), jnp.int32), jnp.cumsum(counts)])
    first_tile = off[:-1] // tm
    last_tile = (off[1:] - 1) // tm
    ntile_e = jnp.where(counts > 0, last_tile - first_tile + 1, 0)
    base_incl = jnp.cumsum(ntile_e)
    base = base_incl - ntile_e
    total = base_incl[-1]
    w = jnp.arange(nw, dtype=jnp.int32)
    wc = jnp.minimum(w, total - 1)
    e = jnp.sum((base_incl[None, :] <= wc[:, None]).astype(jnp.int32), axis=1)
    t = first_tile[e] + (wc - base[e])
    lo = jnp.clip(off[e] - t * tm, 0, tm)
    hi = jnp.clip(off[e + 1] - t * tm, 0, tm)
    valid = w < total
    lo = jnp.where(valid, lo, 0)
    hi = jnp.where(valid, hi, 0)
    prev_t = jnp.concatenate([jnp.full((1,), -1, jnp.int32), t[:-1]])
    first = (valid & (t != prev_t)).astype(jnp.int32)
    return (t.astype(jnp.int32), e, lo.astype(jnp.int32), hi.astype(jnp.int32), first), off


def _combine_kernel(src_ref, tot_ref, ys_ref, aux_ref, tab_ref, x_ref, mod_ref, fg_ref, o_ref,
                    buf, sem, *, final, ntiles):
    i = pl.program_id(0)
    tm = x_ref.shape[0]
    nrow = buf.shape[1]
    slot = i % 2

    def fetch(tile, sl):
        def per_group(gi, c):
            for u in range(FETCH_UNROLL):
                s = gi * FETCH_UNROLL + u
                src_row = pl.multiple_of(src_ref[tile * (nrow // CHUNK_ROWS) + s] * CHUNK_ROWS, CHUNK_ROWS)
                dst_row = pl.multiple_of(s * CHUNK_ROWS, CHUNK_ROWS)
                pltpu.make_async_copy(ys_ref.at[pl.ds(src_row, CHUNK_ROWS), :],
                                      buf.at[sl, pl.ds(dst_row, CHUNK_ROWS), :], sem.at[sl]).start()
            return c
        lax.fori_loop(0, tot_ref[tile] // FETCH_UNROLL, per_group, 0)

    def wait_fetched(tile, sl):
        total = tot_ref[tile]
        for b in range((nrow // CHUNK_ROWS).bit_length()):
            rows = CHUNK_ROWS << b

            @pl.when((total >> b) & 1 == 1)
            def _():
                pltpu.make_async_copy(ys_ref.at[pl.ds(0, rows), :], buf.at[sl, pl.ds(0, rows), :],
                                      sem.at[sl]).wait()

    @pl.when(i == 0)
    def _():
        buf[...] = jnp.zeros_like(buf)
        fetch(0, 0)

    @pl.when(i + 1 < ntiles)
    def _():
        fetch(i + 1, 1 - slot)

    wait_fetched(i, slot)

    aux = aux_ref[...]
    tab = tab_ref[0]
    lane = lax.broadcasted_iota(jnp.int32, (tm, LANES), 1).astype(F32)
    col = lax.broadcasted_iota(jnp.int32, (tm, nrow), 1).astype(F32)
    gmat = jnp.zeros((tm, nrow), F32)
    for r in range(TOP_K):
        base = jnp.sum(jnp.where(lane == aux[:, r:r + 1], tab, 0.0), axis=-1, keepdims=True)
        lrow = base + aux[:, TOP_K + r:TOP_K + r + 1]
        gmat = gmat + jnp.where(col == lrow, aux[:, 2 * TOP_K + r:2 * TOP_K + r + 1], 0.0)
    y = jnp.dot(gmat.astype(BF16), buf[slot], preferred_element_type=F32)
    xn = x_ref[...] + mod_ref[0][5:6] * y
    if final:
        ms = jnp.mean(xn * xn, axis=-1, keepdims=True)
        xn = xn * lax.rsqrt(ms + EPS) * fg_ref[...]
    o_ref[...] = xn


def _combine_plan(off, before, counts, ne, max_chunks):
    start = off[None, :ne] + before
    end = off[None, :ne] + jnp.concatenate([before[1:], counts[None, :]], axis=0)
    c0 = start // CHUNK_ROWS
    cc = jnp.where(end > start, (end + CHUNK_ROWS - 1) // CHUNK_ROWS - c0, 0)
    sb_end = jnp.cumsum(cc, axis=1)
    sb = sb_end - cc
    tab = (sb - c0) * CHUNK_ROWS + off[None, :ne]
    tab = jnp.zeros((before.shape[0], 1, LANES), F32).at[:, 0, :ne].set(tab.astype(F32))
    slot = jnp.arange(max_chunks, dtype=jnp.int32)[None, :, None]
    owner = (sb[:, None, :] <= slot) & (slot < sb_end[:, None, :])
    src = jnp.sum(jnp.where(owner, (c0 - sb)[:, None, :], 0), axis=-1) + slot[:, :, 0]
    flat = lambda a: a.reshape(-1).astype(jnp.int32)
    tot = (sb_end[:, -1] + FETCH_UNROLL - 1) // FETCH_UNROLL * FETCH_UNROLL
    return flat(src), flat(tot), tab


def _combine(ys, plan, aux, xs, mod_l, final_g, nb, n_lat, nrows, final):
    d = xs.shape[1]
    tm = MOE_TILE
    bpb = n_lat // tm
    ntiles = nrows // tm
    src, tot, tab = plan
    max_chunks = src.shape[0] // ntiles
    out_rows = nrows if final else xs.shape[0]
    grid_spec = pltpu.PrefetchScalarGridSpec(
        num_scalar_prefetch=2,
        grid=(ntiles,),
        in_specs=[pl.BlockSpec(memory_space=pl.ANY),
                  pl.BlockSpec((tm, LANES), lambda i, *_: (i, 0)),
                  pl.BlockSpec((1, 1, LANES), lambda i, *_: (i, 0, 0)),
                  pl.BlockSpec((tm, d), lambda i, *_: (i, 0)),
                  pl.BlockSpec((1, 6, d), lambda i, *_: (_group_of(i, bpb, nb), 0, 0)),
                  pl.BlockSpec((1, d), lambda i, *_: (0, 0))],
        out_specs=pl.BlockSpec((tm, d), lambda i, *_: (i, 0)),
        scratch_shapes=[pltpu.VMEM((2, max_chunks * CHUNK_ROWS, d), ys.dtype), pltpu.SemaphoreType.DMA((2,))],
    )
    return pl.pallas_call(
        functools.partial(_combine_kernel, final=final, ntiles=ntiles),
        grid_spec=grid_spec,
        out_shape=jax.ShapeDtypeStruct((out_rows, d), F32),
        input_output_aliases={} if final else {5: 0},
        compiler_params=_cparams(("arbitrary",),
                                 (2 * max_chunks * CHUNK_ROWS * d * 2 + 4 * tm * max_chunks * CHUNK_ROWS * 4 + 8 * tm * d * 4) / 2 ** 20 + 6),
        name="moe_combine",
    )(src, tot, ys, aux, tab, xs, mod_l, final_g.reshape(1, d))


def _moe(xs, mod_l, g, rw, rb, wgu_bf, bgu, wd_all, bd, ebase, final_g, nb, n_lat, nrows, final):
    d = xs.shape[1]
    ne = rw.shape[1]
    rw_pad = jnp.zeros((d, LANES), F32).at[:, :ne].set(rw.astype(F32))
    whi = rw_pad.astype(BF16)
    wlo = (rw_pad - whi.astype(F32)).astype(BF16)
    rb_pad = jnp.full((1, LANES), -1e30, F32).at[0, :ne].set(rb.astype(F32))
    h, aux, cnt, before = _router(xs, mod_l, g, whi, wlo, rb_pad, nb, n_lat, nrows)

    counts = cnt[0, :ne].astype(jnp.int32)
    before = before[:, 0, :ne].astype(jnp.int32)
    tm_e = 512
    p = nrows * TOP_K
    items, off = _expert_items(counts, p, tm_e, ne)
    idx = aux[:, 0:TOP_K].astype(jnp.int32)
    rank = aux[:, TOP_K:2 * TOP_K].astype(jnp.int32)
    pos = (off[idx] + rank).reshape(-1)

    hs = _dispatch(h, pos)
    n2 = bgu.shape[1]
    bgu_p = bgu.astype(F32).reshape(ne, n2 // (2 * LANES), LANES, 2).transpose(0, 1, 3, 2).reshape(ne, n2)
    ys = _experts(hs, items, wgu_bf, bgu_p, wd_all, bd.astype(F32), tm_e, ebase)
    max_chunks = MOE_TILE * TOP_K // CHUNK_ROWS + 2 * ne
    assert max_chunks % FETCH_UNROLL == 0
    plan = _combine_plan(off, before, counts, ne, max_chunks)
    return _combine(ys, plan, aux, xs, mod_l, final_g, nb, n_lat, nrows, final)


def _rope_tables(nb, n_lat, n_ctx):
    def angles(seq_pos, row_pos, col_pos):
        parts = []
        for posv, dim in zip((seq_pos, row_pos, col_pos), ROPE_AXES):
            inv = ROPE_THETA ** (-jnp.arange(0, dim, 2, dtype=F32) / dim)
            parts.append(posv.astype(F32)[:, None] * inv[None, :])
        return jnp.concatenate(parts, axis=-1)

    rows = n_lat // GRID_W
    lat = angles(jnp.full((n_lat,), n_ctx, jnp.int32),
                 jnp.repeat(jnp.arange(rows, dtype=jnp.int32), GRID_W),
                 jnp.tile(jnp.arange(GRID_W, dtype=jnp.int32), rows))
    zl = jnp.zeros((n_ctx,), jnp.int32)
    ctx = angles(jnp.arange(n_ctx, dtype=jnp.int32), zl, zl)
    ang = jnp.concatenate([lat, jnp.tile(ctx, (nb, 1))], axis=0)
    cos, sin = jnp.cos(ang), jnp.sin(ang)
    return jnp.concatenate([cos, cos], axis=-1), jnp.concatenate([-sin, sin], axis=-1)


def _deinterleave_heads(w):
    d, n = w.shape
    w4 = w.reshape(d, n // RET_DK, RET_DK // 2, 2)
    return jnp.concatenate([w4[..., 0], w4[..., 1]], axis=-1).reshape(d, n)


def kernel(x, c, ctx, c_ctx, mod_w, mod_b, norm_mix_g, norm_ffn_g, ev_w_in, ev_w_out, ret_decay_f, ret_decay_b,
           ret_gn_g, conv_w, conv_b, conv_ln_g, conv_ln_b, od_w_in, od_w_out, sgu_ln_g, sgu_ln_b, sgu_w, sgu_b,
           router_w, router_b, moe_w_gu, moe_b_gu, moe_w_down, moe_b_down, final_g):
    nb, n_lat, d = x.shape
    n_ctx = ctx.shape[1]
    depth = mod_w.shape[0]
    n_lat_total = nb * n_lat
    n_ctx_total = nb * n_ctx
    nt = n_lat_total + n_ctx_total
    assert nb < MOD_GROUPS and n_lat % GRID_W == 0
    qk = RET_HEADS * RET_DK

    xs = jnp.concatenate([x.reshape(n_lat_total, d), ctx.reshape(n_ctx_total, d)], axis=0).astype(F32)
    cc = jnp.zeros((MOD_GROUPS, d), F32).at[:nb].set(c.astype(F32)).at[nb].set(c_ctx.astype(F32))
    mod = _mod_table(cc, mod_w, mod_b).reshape(depth, MOD_GROUPS, 6, d)
    cos_t, sin_t = _rope_tables(nb, n_lat, n_ctx)
    ne = router_w.shape[2]
    wgu_bf = _prep_wgu(moe_w_gu.reshape((depth * ne,) + moe_w_gu.shape[2:]))
    wd_all = moe_w_down.reshape((depth * ne,) + moe_w_down.shape[2:])

    out = None
    for layer in range(depth):
        last = layer == depth - 1
        nrows = n_lat_total if last else nt
        mod_l = mod[layer]
        if layer % 2 == 0:
            e = layer // 2
            w_in = ev_w_in[e]
            w_in = jnp.concatenate([_deinterleave_heads(w_in[:, :qk]), _deinterleave_heads(w_in[:, qk:2 * qk]),
                                    w_in[:, 2 * qk:]], axis=1).astype(BF16)
            z = _premix_even(xs, mod_l, norm_mix_g[layer], w_in, cos_t, sin_t, nb, n_lat, n_ctx_total)
            lg = jnp.stack([-jnp.exp(ret_decay_f[e].astype(F32)), -jnp.exp(ret_decay_b[e].astype(F32))])
            gn = ret_gn_g[e].reshape(1, qk).astype(F32)
            s0 = jnp.zeros((nb, RET_HEADS, 2, RET_DK, RET_DK), F32)
            ret0 = jnp.zeros((nt, qk), BF16)
            ret, st = _retention(z, lg, gn, s0, ret0, nb, n_ctx, n_lat_total)
            ret, _ = _retention(z, lg, gn, st, ret, nb, n_lat, 0)
            cv = _conv_module(z, conv_w[e].astype(F32), conv_b[e].astype(F32), conv_ln_g[e].astype(F32),
                              conv_ln_b[e].astype(F32), nb, n_lat, n_ctx)
            xs = _outproj_even(ret, cv, ev_w_out[e].astype(BF16), xs, mod_l, nb, n_lat, n_ctx_total)
        else:
            j = layer // 2
            uv = _premix_odd(xs, mod_l, norm_mix_g[layer], od_w_in[j].astype(BF16), nb, n_lat, nrows)
            sw = od_w_out.shape[1]
            bias_full = jnp.repeat(sgu_b[j].astype(F32).T, sw // SGU_GROUPS, axis=1)
            xs = _sgu_out(uv, sgu_ln_g[j].astype(F32), sgu_ln_b[j].astype(F32), sgu_w[j].astype(BF16), bias_full,
                          od_w_out[j].astype(BF16), xs, mod_l, nb, n_lat, nrows)
        res = _moe(xs, mod_l, norm_ffn_g[layer], router_w[layer], router_b[layer], wgu_bf, moe_b_gu[layer],
                   wd_all, moe_b_down[layer], layer * ne, final_g.astype(F32), nb, n_lat, nrows, last)
        if last:
            out = res
        else:
            xs = res
    return out.reshape(nb, n_lat, d).astype(x.dtype)
```

```python
import functools
import math

import jax
import jax.numpy as jnp
from jax import lax
from jax.experimental import pallas as pl
from jax.experimental.pallas import tpu as pltpu

F32 = jnp.float32
BF16 = jnp.bfloat16

LANES = 128
CHUNK_ROWS = 16
RET_HEADS = 8
RET_DK = 128
RET_CHUNK = 128
ROPE_AXES = (32, 48, 48)
ROPE_THETA = 10000.0
GRID_W = 64
SGU_GROUPS = 8
SGU_CHUNK = 128
TOP_K = 4
MOE_TILE = 256
FETCH_UNROLL = 4
SWIGLU_LIMIT = 7.0
SWIGLU_ALPHA = 1.702
EPS = 1e-6
MOD_GROUPS = 8
HALO = 16
VMEM_CAP = 58 * 2 ** 20


def _cparams(sem, vmem_mb):
    return pltpu.CompilerParams(dimension_semantics=sem,
                                vmem_limit_bytes=min(int(vmem_mb * 2 ** 20), VMEM_CAP))


def _pick_tile(n_lat, n_ctx, cands):
    for t in cands:
        if n_lat % t == 0 and n_ctx % t == 0:
            return t
    raise ValueError("no token tile divides the latent and context lengths")


def _group_of(i, blocks_per_batch, nb):
    return jnp.minimum(i // blocks_per_batch, nb)


def _norm_mod(x, g, shift, scale):
    ms = jnp.mean(x * x, axis=-1, keepdims=True)
    return (x * lax.rsqrt(ms + EPS) * g) * (1.0 + scale) + shift


def _mod_kernel(c_ref, w_ref, b_ref, o_ref):
    c = c_ref[...]
    s = (c * jax.nn.sigmoid(c)).astype(BF16)
    o_ref[0] = jnp.dot(s, w_ref[0].astype(BF16), preferred_element_type=F32) + b_ref[0]


def _mod_table(cc, mod_w, mod_b):
    depth, d, n6 = mod_w.shape
    tn = min(d, 1024)
    return pl.pallas_call(
        _mod_kernel,
        grid=(depth, n6 // tn),
        in_specs=[pl.BlockSpec((MOD_GROUPS, d), lambda l, j: (0, 0)),
                  pl.BlockSpec((1, d, tn), lambda l, j: (l, 0, j)),
                  pl.BlockSpec((1, 1, tn), lambda l, j: (l, 0, j))],
        out_specs=pl.BlockSpec((1, MOD_GROUPS, tn), lambda l, j: (l, 0, j)),
        out_shape=jax.ShapeDtypeStruct((depth, MOD_GROUPS, n6), F32),
        compiler_params=_cparams(("arbitrary", "arbitrary"), 2 * d * tn * 4 / 2 ** 20 + 8),
        name="mod_table",
    )(cc, mod_w, mod_b.reshape(depth, 1, n6))


def _premix_even_kernel(x_ref, mod_ref, g_ref, w_ref, cos_ref, sin_ref, o_ref, h_scr, *, kscale):
    j = pl.program_id(1)

    @pl.when(j == 0)
    def _():
        m = mod_ref[0]
        h_scr[...] = _norm_mod(x_ref[...], g_ref[...], m[0:1], m[1:2]).astype(BF16)

    z = jnp.dot(h_scr[...], w_ref[...], preferred_element_type=F32)

    @pl.when(j < 2)
    def _():
        c = cos_ref[...]
        s = sin_ref[...]
        scale = jnp.where(j == 1, kscale, 1.0).astype(F32)
        for hh in range(z.shape[1] // RET_DK):
            t = z[:, hh * RET_DK:(hh + 1) * RET_DK]
            r = pltpu.roll(t, RET_DK // 2, axis=1)
            o_ref[:, hh * RET_DK:(hh + 1) * RET_DK] = ((t * c + r * s) * scale).astype(BF16)

    @pl.when(j >= 2)
    def _():
        o_ref[...] = z.astype(BF16)


def _premix_even(xs, mod_l, g, w_bf, cos_t, sin_t, nb, n_lat, n_ctx_total):
    nt, d = xs.shape
    n_out = w_bf.shape[1]
    tm = _pick_tile(n_lat, n_ctx_total, (1024, 512, 256))
    tn = RET_HEADS * RET_DK
    bpb = n_lat // tm
    n_lat_blocks = nb * bpb

    def tab_idx(i, j):
        return (jnp.where(i < n_lat_blocks, i % bpb, bpb + (i - n_lat_blocks)), 0)

    return pl.pallas_call(
        functools.partial(_premix_even_kernel, kscale=RET_DK ** -0.5),
        grid=(nt // tm, n_out // tn),
        in_specs=[pl.BlockSpec((tm, d), lambda i, j: (i, 0)),
                  pl.BlockSpec((1, 6, d), lambda i, j: (_group_of(i, bpb, nb), 0, 0)),
                  pl.BlockSpec((1, d), lambda i, j: (0, 0)),
                  pl.BlockSpec((d, tn), lambda i, j: (0, j)),
                  pl.BlockSpec((tm, RET_DK), tab_idx),
                  pl.BlockSpec((tm, RET_DK), tab_idx)],
        out_specs=pl.BlockSpec((tm, tn), lambda i, j: (i, j)),
        out_shape=jax.ShapeDtypeStruct((nt, n_out), BF16),
        scratch_shapes=[pltpu.VMEM((tm, d), BF16)],
        compiler_params=_cparams(("arbitrary", "arbitrary"),
                                 (2 * tm * d * 4 + tm * d * 2 + 2 * d * tn * 2 + 2 * tm * tn * 2
                                  + 2 * tm * tn * 4) / 2 ** 20 + 8),
        name="premix_even",
    )(xs, mod_l, g.reshape(1, d), w_bf, cos_t, sin_t)


def _retention_kernel(lg_ref, q_ref, k_ref, v_ref, g_ref, gn_ref, s0_ref, ret_in_ref, ret_ref, sfin_ref,
                      st_scr, *, nc, unroll):
    del ret_in_ref
    hh = pl.program_id(1)
    lgf = lg_ref[0, hh]
    lgb = lg_ref[1, hh]
    c = RET_CHUNK
    ri = lax.broadcasted_iota(jnp.int32, (c, c), 0).astype(F32)
    ci = lax.broadcasted_iota(jnp.int32, (c, c), 1).astype(F32)
    rel = ri - ci
    dmat = jnp.where(rel >= 0, jnp.exp(lgf * jnp.maximum(rel, 0.0)), jnp.exp(lgb * jnp.maximum(-rel, 0.0)))
    xi_f = jnp.exp(lgf * (ri + 1.0))
    xi_b = jnp.exp(lgb * (c - ri))
    zeta_f = jnp.exp(lgf * (c - 1.0 - ci))
    zeta_b = jnp.exp(lgb * ci)
    gc_f = jnp.exp(lgf * c + jnp.zeros((c, c), F32))
    gc_b = jnp.exp(lgb * c + jnp.zeros((c, c), F32))

    def local_sums(i, carry):
        for u in range(unroll):
            cc = i * unroll + u
            r0 = pl.multiple_of(cc * c, c)
            kt = k_ref[pl.ds(r0, c), :].astype(F32).T
            v = v_ref[pl.ds(r0, c), :]
            st_scr[cc, :, 0:c] = jnp.dot((kt * zeta_f).astype(BF16), v, preferred_element_type=F32)
            st_scr[cc, :, c:2 * c] = jnp.dot((kt * zeta_b).astype(BF16), v, preferred_element_type=F32)
        return carry

    lax.fori_loop(0, nc // unroll, local_sums, 0)

    def scans(i, carry):
        sf, sb = carry
        cb = nc - 1 - i
        uf = st_scr[i, :, 0:c]
        st_scr[i, :, 0:c] = sf
        ub = st_scr[cb, :, c:2 * c]
        st_scr[cb, :, c:2 * c] = sb
        return gc_f * sf + uf, gc_b * sb + ub

    sf, sb = lax.fori_loop(0, nc, scans, (s0_ref[0, 0, 0], s0_ref[0, 0, 1]))
    sfin_ref[0, 0, 0] = sf
    sfin_ref[0, 0, 1] = sb

    def outputs(i, carry):
        for u in range(unroll):
            cc = i * unroll + u
            r0 = pl.multiple_of(cc * c, c)
            q = q_ref[pl.ds(r0, c), :]
            a = lax.dot_general(q, k_ref[pl.ds(r0, c), :], (((1,), (1,)), ((), ())), preferred_element_type=F32)
            o = jnp.dot((a * dmat).astype(BF16), v_ref[pl.ds(r0, c), :], preferred_element_type=F32)
            cr = jnp.dot(q, st_scr[cc].astype(BF16), preferred_element_type=F32)
            o = o + cr[:, 0:c] * xi_f + cr[:, c:2 * c] * xi_b
            mu = jnp.mean(o, axis=-1, keepdims=True)
            dlt = o - mu
            var = jnp.mean(dlt * dlt, axis=-1, keepdims=True)
            y = dlt * lax.rsqrt(var + EPS) * gn_ref[...]
            g = g_ref[pl.ds(r0, c), :].astype(F32)
            ret_ref[pl.ds(r0, c), :] = (g * jax.nn.sigmoid(g) * y).astype(BF16)
        return carry

    lax.fori_loop(0, nc // unroll, outputs, 0)


def _retention(z, lg, gn, s0, ret_prev, nb, seq, row0):
    nt = z.shape[0]
    nc = seq // RET_CHUNK
    rb0 = row0 // seq
    hcols = RET_HEADS

    def col(off):
        return lambda b, h: (rb0 + b, off * hcols + h)

    ret, sfin = pl.pallas_call(
        functools.partial(_retention_kernel, nc=nc, unroll=max(u for u in (16, 8, 4, 2, 1) if nc % u == 0)),
        grid=(nb, RET_HEADS),
        in_specs=[pl.BlockSpec(memory_space=pltpu.SMEM),
                  pl.BlockSpec((seq, RET_DK), col(0)),
                  pl.BlockSpec((seq, RET_DK), col(1)),
                  pl.BlockSpec((seq, RET_DK), col(2)),
                  pl.BlockSpec((seq, RET_DK), col(3)),
                  pl.BlockSpec((1, RET_DK), lambda b, h: (0, h)),
                  pl.BlockSpec((1, 1, 2, RET_DK, RET_DK), lambda b, h: (b, h, 0, 0, 0)),
                  pl.BlockSpec(memory_space=pl.ANY)],
        out_specs=[pl.BlockSpec((seq, RET_DK), lambda b, h: (rb0 + b, h)),
                   pl.BlockSpec((1, 1, 2, RET_DK, RET_DK), lambda b, h: (b, h, 0, 0, 0))],
        out_shape=[jax.ShapeDtypeStruct((nt, RET_HEADS * RET_DK), BF16),
                   jax.ShapeDtypeStruct((nb, RET_HEADS, 2, RET_DK, RET_DK), F32)],
        scratch_shapes=[pltpu.VMEM((nc, RET_DK, 2 * RET_DK), F32)],
        input_output_aliases={7: 0},
        compiler_params=_cparams(("arbitrary", "arbitrary"),
                                 (10 * seq * RET_DK * 2 + 2 * nc * RET_DK * RET_DK * 4) / 2 ** 20 + 8),
        name="retention",
    )(lg, z, z, z, z, gn, s0, ret_prev)
    return ret, sfin


def _conv_kernel(a_ref, b_ref, ap_ref, bp_ref, an_ref, bn_ref, w_ref, cb_ref, lg_ref, lb_ref, o_ref, ext, shifted,
                 *, tm, n_lat_total, n_lat, n_ctx, ktaps):
    i = pl.program_id(0)
    row0 = i * tm
    is_lat = row0 < n_lat_total
    pos = jnp.where(is_lat, row0 % n_lat, (row0 - n_lat_total) % n_ctx)
    seq = jnp.where(is_lat, n_lat, n_ctx)
    keep_prev = (pos != 0).astype(F32)
    keep_next = (pos + tm != seq).astype(F32)

    def glu(a, b):
        return a.astype(F32) * jax.nn.sigmoid(b.astype(F32))

    ext[0:HALO, :] = glu(ap_ref[...], bp_ref[...]) * keep_prev
    ext[HALO:HALO + tm, :] = glu(a_ref[...], b_ref[...])
    ext[HALO + tm:HALO + tm + HALO, :] = glu(an_ref[...], bn_ref[...]) * keep_next

    half = ktaps // 2
    rc = 32
    sub = 8
    span = shifted.shape[1]
    for s in range(1, sub):
        shifted[s - 1] = ext[s:s + span, :]
    for r in range(tm // rc):
        acc = jnp.zeros((rc, a_ref.shape[1]), F32)
        for j in range(ktaps):
            o = HALO + r * rc + j - half
            s, base = o % sub, o - o % sub
            win = ext[base:base + rc, :] if s == 0 else shifted[s - 1, base:base + rc, :]
            acc = acc + w_ref[j:j + 1, :] * win
        y = acc + cb_ref[...]
        mu = jnp.mean(y, axis=-1, keepdims=True)
        dlt = y - mu
        var = jnp.mean(dlt * dlt, axis=-1, keepdims=True)
        yn = dlt * lax.rsqrt(var + EPS) * lg_ref[...] + lb_ref[...]
        o_ref[r * rc:(r + 1) * rc, :] = (yn * jax.nn.sigmoid(yn)).astype(BF16)


def _conv_module(z, conv_w, conv_b, ln_g, ln_b, nb, n_lat, n_ctx):
    nt = z.shape[0]
    ktaps, cw = conv_w.shape
    tm = 256
    assert n_lat % tm == 0 and n_ctx % tm == 0 and ktaps // 2 <= HALO
    a_blk = (2 * RET_HEADS * RET_DK + 2 * RET_HEADS * RET_DK) // cw
    hb = tm // HALO
    last = nt // HALO - 1
    kern = functools.partial(_conv_kernel, tm=tm, n_lat_total=nb * n_lat, n_lat=n_lat, n_ctx=n_ctx, ktaps=ktaps)
    vec = lambda i: (0, 0)
    return pl.pallas_call(
        kern,
        grid=(nt // tm,),
        in_specs=[pl.BlockSpec((tm, cw), lambda i: (i, a_blk)),
                  pl.BlockSpec((tm, cw), lambda i: (i, a_blk + 1)),
                  pl.BlockSpec((HALO, cw), lambda i: (jnp.maximum(i * hb - 1, 0), a_blk)),
                  pl.BlockSpec((HALO, cw), lambda i: (jnp.maximum(i * hb - 1, 0), a_blk + 1)),
                  pl.BlockSpec((HALO, cw), lambda i: (jnp.minimum((i + 1) * hb, last), a_blk)),
                  pl.BlockSpec((HALO, cw), lambda i: (jnp.minimum((i + 1) * hb, last), a_blk + 1)),
                  pl.BlockSpec((ktaps, cw), vec),
                  pl.BlockSpec((1, cw), vec),
                  pl.BlockSpec((1, cw), vec),
                  pl.BlockSpec((1, cw), vec)],
        out_specs=pl.BlockSpec((tm, cw), lambda i: (i, 0)),
        out_shape=jax.ShapeDtypeStruct((nt, cw), BF16),
        scratch_shapes=[pltpu.VMEM((tm + 2 * HALO, cw), F32),
                        pltpu.VMEM((7, tm + 2 * HALO - 8, cw), F32)],
        compiler_params=_cparams(("arbitrary",), 40),
        name="conv_module",
    )(z, z, z, z, z, z, conv_w, conv_b.reshape(1, cw), ln_g.reshape(1, cw), ln_b.reshape(1, cw))


def _outproj_even_kernel(ret_ref, cv_ref, w_ref, x_ref, mod_ref, o_ref):
    rw = ret_ref.shape[1]
    y = jnp.dot(ret_ref[...], w_ref[0:rw, :], preferred_element_type=F32)
    y = y + jnp.dot(cv_ref[...], w_ref[rw:, :], preferred_element_type=F32)
    o_ref[...] = x_ref[...] + mod_ref[0][2:3] * y


def _outproj_even(ret, cv, w_bf, xs, mod_l, nb, n_lat, n_ctx_total):
    nt, d = xs.shape
    tm = _pick_tile(n_lat, n_ctx_total, (512, 256))
    bpb = n_lat // tm
    rw, cw = ret.shape[1], cv.shape[1]
    return pl.pallas_call(
        _outproj_even_kernel,
        grid=(nt // tm,),
        in_specs=[pl.BlockSpec((tm, rw), lambda i: (i, 0)),
                  pl.BlockSpec((tm, cw), lambda i: (i, 0)),
                  pl.BlockSpec((rw + cw, d), lambda i: (0, 0)),
                  pl.BlockSpec((tm, d), lambda i: (i, 0)),
                  pl.BlockSpec((1, 6, d), lambda i: (_group_of(i, bpb, nb), 0, 0))],
        out_specs=pl.BlockSpec((tm, d), lambda i: (i, 0)),
        out_shape=jax.ShapeDtypeStruct((nt, d), F32),
        input_output_aliases={3: 0},
        compiler_params=_cparams(("arbitrary",),
                                 (2 * (rw + cw) * d * 2 + 4 * tm * d * 4 + 2 * tm * (rw + cw) * 2
                                  + tm * d * 4) / 2 ** 20 + 8),
        name="outproj_even",
    )(ret, cv, w_bf, xs, mod_l)


def _premix_odd_kernel(x_ref, mod_ref, g_ref, w_ref, o_ref, h_scr):
    j = pl.program_id(1)

    @pl.when(j == 0)
    def _():
        m = mod_ref[0]
        h_scr[...] = _norm_mod(x_ref[...], g_ref[...], m[0:1], m[1:2]).astype(BF16)

    z = jnp.dot(h_scr[...], w_ref[...], preferred_element_type=F32)
    o_ref[...] = (0.5 * z * (1.0 + lax.erf(z * (2.0 ** -0.5)))).astype(BF16)


def _premix_odd(xs, mod_l, g, w_bf, nb, n_lat, nrows):
    d = xs.shape[1]
    n_out = w_bf.shape[1]
    tm = _pick_tile(n_lat, nrows, (1024, 512, 256))
    tn = 1024
    bpb = n_lat // tm
    return pl.pallas_call(
        _premix_odd_kernel,
        grid=(nrows // tm, n_out // tn),
        in_specs=[pl.BlockSpec((tm, d), lambda i, j: (i, 0)),
                  pl.BlockSpec((1, 6, d), lambda i, j: (_group_of(i, bpb, nb), 0, 0)),
                  pl.BlockSpec((1, d), lambda i, j: (0, 0)),
                  pl.BlockSpec((d, tn), lambda i, j: (0, j))],
        out_specs=pl.BlockSpec((tm, tn), lambda i, j: (i, j)),
        out_shape=jax.ShapeDtypeStruct((nrows, n_out), BF16),
        scratch_shapes=[pltpu.VMEM((tm, d), BF16)],
        compiler_params=_cparams(("arbitrary", "arbitrary"),
                                 (2 * tm * d * 4 + tm * d * 2 + 2 * d * tn * 2 + 2 * tm * tn * 2
                                  + 2 * tm * tn * 4) / 2 ** 20 + 8),
        name="premix_odd",
    )(xs, mod_l, g.reshape(1, d), w_bf)


def _sgu_out_kernel(u_ref, v_ref, lg_ref, lb_ref, ws_ref, bs_ref, w_ref, x_ref, mod_ref, o_ref, p_scr):
    tm, sw = u_ref.shape
    gw = sw // SGU_GROUPS
    v = v_ref[...].astype(F32)
    mu = jnp.mean(v, axis=-1, keepdims=True)
    dlt = v - mu
    var = jnp.mean(dlt * dlt, axis=-1, keepdims=True)
    p_scr[...] = (dlt * lax.rsqrt(var + EPS) * lg_ref[...] + lb_ref[...]).astype(BF16)
    for cc in range(tm // SGU_CHUNK):
        rows = slice(cc * SGU_CHUNK, (cc + 1) * SGU_CHUNK)
        for gi in range(SGU_GROUPS):
            cols = slice(gi * gw, (gi + 1) * gw)
            mixed = jnp.dot(ws_ref[gi], p_scr[rows, cols], preferred_element_type=F32) + bs_ref[:, cols]
            p_scr[rows, cols] = (u_ref[rows, cols].astype(F32) * mixed).astype(BF16)
    y = jnp.dot(p_scr[...], w_ref[...], preferred_element_type=F32)
    o_ref[...] = x_ref[...] + mod_ref[0][2:3] * y


def _sgu_out(uv, ln_g, ln_b, ws_bf, bias_full, w_bf, xs, mod_l, nb, n_lat, nrows):
    nt, d = xs.shape
    sw = w_bf.shape[0]
    tm = _pick_tile(n_lat, nrows, (512, 256))
    bpb = n_lat // tm
    vec = lambda i: (0, 0)
    return pl.pallas_call(
        _sgu_out_kernel,
        grid=(nrows // tm,),
        in_specs=[pl.BlockSpec((tm, sw), lambda i: (i, 0)),
                  pl.BlockSpec((tm, sw), lambda i: (i, 1)),
                  pl.BlockSpec((1, sw), vec),
                  pl.BlockSpec((1, sw), vec),
                  pl.BlockSpec((SGU_GROUPS, SGU_CHUNK, SGU_CHUNK), lambda i: (0, 0, 0)),
                  pl.BlockSpec((SGU_CHUNK, sw), vec),
                  pl.BlockSpec((sw, d), vec),
                  pl.BlockSpec((tm, d), lambda i: (i, 0)),
                  pl.BlockSpec((1, 6, d), lambda i: (_group_of(i, bpb, nb), 0, 0))],
        out_specs=pl.BlockSpec((tm, d), lambda i: (i, 0)),
        out_shape=jax.ShapeDtypeStruct((nt, d), F32),
        scratch_shapes=[pltpu.VMEM((tm, sw), BF16)],
        input_output_aliases={7: 0},
        compiler_params=_cparams(("arbitrary",),
                                 (2 * sw * d * 2 + 4 * tm * d * 4 + 4 * tm * sw * 2 + tm * sw * 2
                                  + 2 * SGU_CHUNK * sw * 4 + 2 * tm * sw * 4) / 2 ** 20 + 8),
        name="sgu_out",
    )(uv, uv, ln_g.reshape(1, sw), ln_b.reshape(1, sw), ws_bf, bias_full, w_bf, xs, mod_l)


def _router_kernel(x_ref, mod_ref, g_ref, w_ref, rb_ref, h_ref, aux_ref, cnt_ref, before_ref, carry):
    i = pl.program_id(0)
    tm = x_ref.shape[0]

    @pl.when(i == 0)
    def _():
        carry[...] = jnp.zeros_like(carry)

    m = mod_ref[0]
    h = _norm_mod(x_ref[...], g_ref[...], m[3:4], m[4:5])
    h_ref[...] = h
    h_hi = h.astype(BF16)
    h_lo = (h - h_hi.astype(F32)).astype(BF16)
    p_hi = jnp.dot(h_hi, w_ref[...], preferred_element_type=F32)
    p_lo = jnp.dot(h_lo, w_ref[:, 0:LANES], preferred_element_type=F32)
    logits = p_hi[:, 0:LANES] + p_lo + p_hi[:, LANES:2 * LANES] + rb_ref[...]

    lane = lax.broadcasted_iota(jnp.int32, (tm, LANES), 1).astype(F32)
    vals, idxs = [], []
    l = logits
    for _ in range(TOP_K):
        mx = jnp.max(l, axis=-1, keepdims=True)
        ix = jnp.min(jnp.where(l == mx, lane, float(LANES)), axis=-1, keepdims=True)
        vals.append(mx)
        idxs.append(ix)
        l = jnp.where(lane == ix, -jnp.inf, l)
    es = [jnp.exp(v - vals[0]) for v in vals]
    den = es[0]
    for e in es[1:]:
        den = den + e

    sel = jnp.zeros((tm, LANES), F32)
    for ix in idxs:
        sel = sel + jnp.where(lane == ix, 1.0, 0.0)
    rr = lax.broadcasted_iota(jnp.int32, (tm, tm), 0)
    rc = lax.broadcasted_iota(jnp.int32, (tm, tm), 1)
    ltri = jnp.where(rc < rr, 1.0, 0.0).astype(BF16)
    rank = jnp.dot(ltri, sel.astype(BF16), preferred_element_type=F32) + carry[...]
    before_ref[0] = jnp.broadcast_to(carry[...], before_ref.shape[1:])
    carry[...] = carry[...] + jnp.sum(sel, axis=0, keepdims=True)
    cnt_ref[...] = carry[...]

    aux = jnp.zeros((tm, LANES), F32)
    for r in range(TOP_K):
        rk = jnp.sum(jnp.where(lane == idxs[r], rank, 0.0), axis=-1, keepdims=True)
        aux = jnp.where(lane == float(r), idxs[r], aux)
        aux = jnp.where(lane == float(TOP_K + r), rk, aux)
        aux = jnp.where(lane == float(2 * TOP_K + r), es[r] / den, aux)
    aux_ref[...] = aux


def _router(xs, mod_l, g, w_hilo, rb, nb, n_lat, nrows):
    d = xs.shape[1]
    tm = MOE_TILE
    bpb = n_lat // tm
    vec = lambda i: (0, 0)
    return pl.pallas_call(
        _router_kernel,
        grid=(nrows // tm,),
        in_specs=[pl.BlockSpec((tm, d), lambda i: (i, 0)),
                  pl.BlockSpec((1, 6, d), lambda i: (_group_of(i, bpb, nb), 0, 0)),
                  pl.BlockSpec((1, d), vec),
                  pl.BlockSpec((d, 2 * LANES), vec),
                  pl.BlockSpec((1, LANES), vec)],
        out_specs=[pl.BlockSpec((tm, d), lambda i: (i, 0)),
                   pl.BlockSpec((tm, LANES), lambda i: (i, 0)),
                   pl.BlockSpec((1, LANES), vec),
                   pl.BlockSpec((1, 8, LANES), lambda i: (i, 0, 0))],
        out_shape=[jax.ShapeDtypeStruct((nrows, d), F32),
                   jax.ShapeDtypeStruct((nrows, LANES), F32),
                   jax.ShapeDtypeStruct((1, LANES), F32),
                   jax.ShapeDtypeStruct((nrows // tm, 8, LANES), F32)],
        scratch_shapes=[pltpu.VMEM((1, LANES), F32)],
        compiler_params=_cparams(("arbitrary",), 32),
        name="router",
    )(xs, mod_l, g.reshape(1, d), w_hilo, rb)


def _dispatch_kernel(pos_ref, h_ref, hs_ref, sem):
    tm = h_ref.shape[0]

    def row_copy(t, p):
        return pltpu.make_async_copy(h_ref.at[pl.ds(t, 1), :], hs_ref.at[pl.ds(p, 1), :], sem)

    def issue(t, carry):
        for r in range(TOP_K):
            row_copy(t, pos_ref[0, 0, t * TOP_K + r]).start()
        return carry

    lax.fori_loop(0, tm, issue, 0)
    for _ in range(TOP_K):
        pltpu.make_async_copy(h_ref, hs_ref.at[pl.ds(0, tm), :], sem).wait()


def _dispatch(h, pos):
    nrows, d = h.shape
    tm = MOE_TILE
    pos3 = pos.reshape(nrows // tm, 1, tm * TOP_K)
    return pl.pallas_call(
        _dispatch_kernel,
        grid=(nrows // tm,),
        in_specs=[pl.BlockSpec((1, 1, tm * TOP_K), lambda i: (i, 0, 0), memory_space=pltpu.SMEM),
                  pl.BlockSpec((tm, d), lambda i: (i, 0))],
        out_specs=pl.BlockSpec(memory_space=pl.ANY),
        out_shape=jax.ShapeDtypeStruct((nrows * TOP_K, d), F32),
        scratch_shapes=[pltpu.SemaphoreType.DMA(())],
        compiler_params=_cparams(("arbitrary",), 16),
        name="moe_dispatch",
    )(pos3, h)


def _pair_perm():
    i = jnp.arange(2 * LANES)[:, None]
    j = jnp.arange(2 * LANES)[None, :]
    return jnp.where(j < LANES, i == 2 * j, i == 2 * (j - LANES) + 1).astype(BF16)


def _prep_wgu_kernel(w_ref, p_ref, o_ref):
    blk = 2 * LANES
    for b in range(w_ref.shape[2] // blk):
        cols = slice(b * blk, (b + 1) * blk)
        o_ref[0, :, cols] = jnp.dot(w_ref[0, :, cols].astype(BF16), p_ref[...],
                                    preferred_element_type=F32).astype(BF16)


def _prep_wgu(wgu):
    lead = wgu.shape[:-2]
    d, n2 = wgu.shape[-2:]
    ne = math.prod(lead)
    tr = 1024 if d % 1024 == 0 else d
    out = pl.pallas_call(
        _prep_wgu_kernel,
        grid=(ne, d // tr),
        in_specs=[pl.BlockSpec((1, tr, n2), lambda e, r: (e, r, 0)),
                  pl.BlockSpec((2 * LANES, 2 * LANES), lambda e, r: (0, 0))],
        out_specs=pl.BlockSpec((1, tr, n2), lambda e, r: (e, r, 0)),
        out_shape=jax.ShapeDtypeStruct((ne, d, n2), BF16),
        compiler_params=_cparams(("arbitrary", "arbitrary"), 3 * tr * n2 * 4 / 2 ** 20 + 8),
        name="prep_wgu",
    )(wgu.reshape(ne, d, n2), _pair_perm())
    return out.reshape(lead + (d, n2))


def _expert_kernel(tile_ref, exp_ref, lo_ref, hi_ref, first_ref, hs_ref, wgu_ref, bgu_ref, wd_ref, bd_ref, o_ref):
    del tile_ref, exp_ref
    w = pl.program_id(0)
    tm = hs_ref.shape[0]
    lo, hi, first = lo_ref[w], hi_ref[w], first_ref[w]

    def rows_block(r0, nr):
        rows = slice(r0, r0 + nr)
        z = jnp.dot(hs_ref[rows, :].astype(BF16), wgu_ref[0], preferred_element_type=F32) + bgu_ref[0]
        parts = []
        for b in range(z.shape[1] // (2 * LANES)):
            zg = jnp.minimum(z[:, 2 * b * LANES:(2 * b + 1) * LANES], SWIGLU_LIMIT)
            zu = jnp.clip(z[:, (2 * b + 1) * LANES:(2 * b + 2) * LANES], -SWIGLU_LIMIT, SWIGLU_LIMIT)
            parts.append(((zu + 1.0) * (zg * jax.nn.sigmoid(SWIGLU_ALPHA * zg))).astype(BF16))
        act = jnp.concatenate(parts, axis=1)
        y = jnp.dot(act, wd_ref[0].astype(BF16), preferred_element_type=F32) + bd_ref[0]
        y = y.astype(o_ref.dtype)
        row = r0 + lax.broadcasted_iota(jnp.int32, (nr, 1), 0)
        mine = (row >= lo) & (row < hi)

        @pl.when(first == 1)
        def _():
            o_ref[rows, :] = jnp.where(mine, y, jnp.zeros_like(y))

        @pl.when(first == 0)
        def _():
            o_ref[rows, :] = jnp.where(mine, y, o_ref[rows, :])

    whole = (lo == 0) & (hi == tm)

    @pl.when(whole)
    def _():
        rows_block(0, tm)

    @pl.when(jnp.logical_not(whole))
    def _():
        half = tm // 2
        for r0 in (0, half):
            touched = (lo < r0 + half) & (hi > r0)

            @pl.when(touched)
            def _():
                rows_block(r0, half)

            @pl.when(jnp.logical_not(touched) & (first == 1))
            def _():
                o_ref[r0:r0 + half, :] = jnp.zeros((half, o_ref.shape[1]), o_ref.dtype)


def _experts(hs, items, wgu_bf, bgu, wd_all, bd, tm, ebase):
    p, d = hs.shape
    n2 = wgu_bf.shape[2]
    ne = bgu.shape[0]
    de = n2 // 2
    nw = items[0].shape[0]
    grid_spec = pltpu.PrefetchScalarGridSpec(
        num_scalar_prefetch=5,
        grid=(nw,),
        in_specs=[pl.BlockSpec((tm, d), lambda w, t, e, lo, hi, f: (t[w], 0)),
                  pl.BlockSpec((1, d, n2), lambda w, t, e, lo, hi, f: (ebase + e[w], 0, 0)),
                  pl.BlockSpec((1, 1, n2), lambda w, t, e, lo, hi, f: (e[w], 0, 0)),
                  pl.BlockSpec((1, de, d), lambda w, t, e, lo, hi, f: (ebase + e[w], 0, 0)),
                  pl.BlockSpec((1, 1, d), lambda w, t, e, lo, hi, f: (e[w], 0, 0))],
        out_specs=pl.BlockSpec((tm, d), lambda w, t, e, lo, hi, f: (t[w], 0)),
    )
    return pl.pallas_call(
        _expert_kernel,
        grid_spec=grid_spec,
        out_shape=jax.ShapeDtypeStruct((p, d), BF16),
        compiler_params=_cparams(("arbitrary",),
                                 (2 * tm * d * 4 + 2 * tm * d * 2 + 2 * d * n2 * 2 + 2 * de * d * 4 + de * d * 2
                                  + 2 * tm * n2 * 4 + tm * d * 4) / 2 ** 20 + 6),
        name="moe_experts",
    )(*items, hs, wgu_bf, bgu.reshape(ne, 1, n2), wd_all, bd.reshape(ne, 1, d))


def _expert_items(counts, p, tm, ne):
    ntiles = p // tm
    nw = ntiles + ne - 1
    off = jnp.concatenate([jnp.zeros((1,), jnp.int32), jnp.cumsum(counts)])
    first_tile = off[:-1] // tm
    last_tile = (off[1:] - 1) // tm
    ntile_e = jnp.where(counts > 0, last_tile - first_tile + 1, 0)
    base_incl = jnp.cumsum(ntile_e)
    base = base_incl - ntile_e
    total = base_incl[-1]
    w = jnp.arange(nw, dtype=jnp.int32)
    wc = jnp.minimum(w, total - 1)
    e = jnp.sum((base_incl[None, :] <= wc[:, None]).astype(jnp.int32), axis=1)
    t = first_tile[e] + (wc - base[e])
    lo = jnp.clip(off[e] - t * tm, 0, tm)
    hi = jnp.clip(off[e + 1] - t * tm, 0, tm)
    valid = w < total
    lo = jnp.where(valid, lo, 0)
    hi = jnp.where(valid, hi, 0)
    prev_t = jnp.concatenate([jnp.full((1,), -1, jnp.int32), t[:-1]])
    first = (valid & (t != prev_t)).astype(jnp.int32)
    return (t.astype(jnp.int32), e, lo.astype(jnp.int32), hi.astype(jnp.int32), first), off


def _combine_kernel(src_ref, tot_ref, ys_ref, aux_ref, tab_ref, x_ref, mod_ref, fg_ref, o_ref,
                    buf, sem, *, final, ntiles):
    i = pl.program_id(0)
    tm = x_ref.shape[0]
    nrow = buf.shape[1]
    slot = i % 2

    def fetch(tile, sl):
        def per_group(gi, c):
            for u in range(FETCH_UNROLL):
                s = gi * FETCH_UNROLL + u
                src_row = pl.multiple_of(src_ref[tile * (nrow // CHUNK_ROWS) + s] * CHUNK_ROWS, CHUNK_ROWS)
                dst_row = pl.multiple_of(s * CHUNK_ROWS, CHUNK_ROWS)
                pltpu.make_async_copy(ys_ref.at[pl.ds(src_row, CHUNK_ROWS), :],
                                      buf.at[sl, pl.ds(dst_row, CHUNK_ROWS), :], sem.at[sl]).start()
            return c
        lax.fori_loop(0, tot_ref[tile] // FETCH_UNROLL, per_group, 0)

    def wait_fetched(tile, sl):
        total = tot_ref[tile]
        for b in range((nrow // CHUNK_ROWS).bit_length()):
            rows = CHUNK_ROWS << b

            @pl.when((total >> b) & 1 == 1)
            def _():
                pltpu.make_async_copy(ys_ref.at[pl.ds(0, rows), :], buf.at[sl, pl.ds(0, rows), :],
                                      sem.at[sl]).wait()

    @pl.when(i == 0)
    def _():
        buf[...] = jnp.zeros_like(buf)
        fetch(0, 0)

    @pl.when(i + 1 < ntiles)
    def _():
        fetch(i + 1, 1 - slot)

    wait_fetched(i, slot)

    aux = aux_ref[...]
    tab = tab_ref[0]
    lane = lax.broadcasted_iota(jnp.int32, (tm, LANES), 1).astype(F32)
    col = lax.broadcasted_iota(jnp.int32, (tm, nrow), 1).astype(F32)
    gmat = jnp.zeros((tm, nrow), F32)
    for r in range(TOP_K):
        base = jnp.sum(jnp.where(lane == aux[:, r:r + 1], tab, 0.0), axis=-1, keepdims=True)
        lrow = base + aux[:, TOP_K + r:TOP_K + r + 1]
        gmat = gmat + jnp.where(col == lrow, aux[:, 2 * TOP_K + r:2 * TOP_K + r + 1], 0.0)
    y = jnp.dot(gmat.astype(BF16), buf[slot], preferred_element_type=F32)
    xn = x_ref[...] + mod_ref[0][5:6] * y
    if final:
        ms = jnp.mean(xn * xn, axis=-1, keepdims=True)
        xn = xn * lax.rsqrt(ms + EPS) * fg_ref[...]
    o_ref[...] = xn


def _combine_plan(off, before, counts, ne, max_chunks):
    start = off[None, :ne] + before
    end = off[None, :ne] + jnp.concatenate([before[1:], counts[None, :]], axis=0)
    c0 = start // CHUNK_ROWS
    cc = jnp.where(end > start, (end + CHUNK_ROWS - 1) // CHUNK_ROWS - c0, 0)
    sb_end = jnp.cumsum(cc, axis=1)
    sb = sb_end - cc
    tab = (sb - c0) * CHUNK_ROWS + off[None, :ne]
    tab = jnp.zeros((before.shape[0], 1, LANES), F32).at[:, 0, :ne].set(tab.astype(F32))
    slot = jnp.arange(max_chunks, dtype=jnp.int32)[None, :, None]
    owner = (sb[:, None, :] <= slot) & (slot < sb_end[:, None, :])
    src = jnp.sum(jnp.where(owner, (c0 - sb)[:, None, :], 0), axis=-1) + slot[:, :, 0]
    flat = lambda a: a.reshape(-1).astype(jnp.int32)
    tot = (sb_end[:, -1] + FETCH_UNROLL - 1) // FETCH_UNROLL * FETCH_UNROLL
    return flat(src), flat(tot), tab


def _combine(ys, plan, aux, xs, mod_l, final_g, nb, n_lat, nrows, final):
    d = xs.shape[1]
    tm = MOE_TILE
    bpb = n_lat // tm
    ntiles = nrows // tm
    src, tot, tab = plan
    max_chunks = src.shape[0] // ntiles
    out_rows = nrows if final else xs.shape[0]
    grid_spec = pltpu.PrefetchScalarGridSpec(
        num_scalar_prefetch=2,
        grid=(ntiles,),
        in_specs=[pl.BlockSpec(memory_space=pl.ANY),
                  pl.BlockSpec((tm, LANES), lambda i, *_: (i, 0)),
                  pl.BlockSpec((1, 1, LANES), lambda i, *_: (i, 0, 0)),
                  pl.BlockSpec((tm, d), lambda i, *_: (i, 0)),
                  pl.BlockSpec((1, 6, d), lambda i, *_: (_group_of(i, bpb, nb), 0, 0)),
                  pl.BlockSpec((1, d), lambda i, *_: (0, 0))],
        out_specs=pl.BlockSpec((tm, d), lambda i, *_: (i, 0)),
        scratch_shapes=[pltpu.VMEM((2, max_chunks * CHUNK_ROWS, d), ys.dtype), pltpu.SemaphoreType.DMA((2,))],
    )
    return pl.pallas_call(
        functools.partial(_combine_kernel, final=final, ntiles=ntiles),
        grid_spec=grid_spec,
        out_shape=jax.ShapeDtypeStruct((out_rows, d), F32),
        input_output_aliases={} if final else {5: 0},
        compiler_params=_cparams(("arbitrary",),
                                 (2 * max_chunks * CHUNK_ROWS * d * 2 + 4 * tm * max_chunks * CHUNK_ROWS * 4 + 8 * tm * d * 4) / 2 ** 20 + 6),
        name="moe_combine",
    )(src, tot, ys, aux, tab, xs, mod_l, final_g.reshape(1, d))


def _moe(xs, mod_l, g, rw, rb, wgu_bf, bgu, wd_all, bd, ebase, final_g, nb, n_lat, nrows, final):
    d = xs.shape[1]
    ne = rw.shape[1]
    rw_pad = jnp.zeros((d, LANES), F32).at[:, :ne].set(rw.astype(F32))
    whi = rw_pad.astype(BF16)
    wlo = (rw_pad - whi.astype(F32)).astype(BF16)
    rb_pad = jnp.full((1, LANES), -1e30, F32).at[0, :ne].set(rb.astype(F32))
    h, aux, cnt, before = _router(xs, mod_l, g, jnp.concatenate([whi, wlo], axis=1), rb_pad, nb, n_lat, nrows)

    counts = cnt[0, :ne].astype(jnp.int32)
    before = before[:, 0, :ne].astype(jnp.int32)
    tm_e = 512
    p = nrows * TOP_K
    items, off = _expert_items(counts, p, tm_e, ne)
    idx = aux[:, 0:TOP_K].astype(jnp.int32)
    rank = aux[:, TOP_K:2 * TOP_K].astype(jnp.int32)
    pos = (off[idx] + rank).reshape(-1)

    hs = _dispatch(h, pos)
    n2 = bgu.shape[1]
    bgu_p = bgu.astype(F32).reshape(ne, n2 // (2 * LANES), LANES, 2).transpose(0, 1, 3, 2).reshape(ne, n2)
    ys = _experts(hs, items, wgu_bf, bgu_p, wd_all, bd.astype(F32), tm_e, ebase)
    max_chunks = MOE_TILE * TOP_K // CHUNK_ROWS + 2 * ne
    assert max_chunks % FETCH_UNROLL == 0
    plan = _combine_plan(off, before, counts, ne, max_chunks)
    return _combine(ys, plan, aux, xs, mod_l, final_g, nb, n_lat, nrows, final)


def _rope_tables(nb, n_lat, n_ctx):
    def angles(seq_pos, row_pos, col_pos):
        parts = []
        for posv, dim in zip((seq_pos, row_pos, col_pos), ROPE_AXES):
            inv = ROPE_THETA ** (-jnp.arange(0, dim, 2, dtype=F32) / dim)
            parts.append(posv.astype(F32)[:, None] * inv[None, :])
        return jnp.concatenate(parts, axis=-1)

    rows = n_lat // GRID_W
    lat = angles(jnp.full((n_lat,), n_ctx, jnp.int32),
                 jnp.repeat(jnp.arange(rows, dtype=jnp.int32), GRID_W),
                 jnp.tile(jnp.arange(GRID_W, dtype=jnp.int32), rows))
    zl = jnp.zeros((n_ctx,), jnp.int32)
    ctx = angles(jnp.arange(n_ctx, dtype=jnp.int32), zl, zl)
    ang = jnp.concatenate([lat, jnp.tile(ctx, (nb, 1))], axis=0)
    cos, sin = jnp.cos(ang), jnp.sin(ang)
    return jnp.concatenate([cos, cos], axis=-1), jnp.concatenate([-sin, sin], axis=-1)


def _deinterleave_heads(w):
    d, n = w.shape
    w4 = w.reshape(d, n // RET_DK, RET_DK // 2, 2)
    return jnp.concatenate([w4[..., 0], w4[..., 1]], axis=-1).reshape(d, n)


def kernel(x, c, ctx, c_ctx, mod_w, mod_b, norm_mix_g, norm_ffn_g, ev_w_in, ev_w_out, ret_decay_f, ret_decay_b,
           ret_gn_g, conv_w, conv_b, conv_ln_g, conv_ln_b, od_w_in, od_w_out, sgu_ln_g, sgu_ln_b, sgu_w, sgu_b,
           router_w, router_b, moe_w_gu, moe_b_gu, moe_w_down, moe_b_down, final_g):
    nb, n_lat, d = x.shape
    n_ctx = ctx.shape[1]
    depth = mod_w.shape[0]
    n_lat_total = nb * n_lat
    n_ctx_total = nb * n_ctx
    nt = n_lat_total + n_ctx_total
    assert nb < MOD_GROUPS and n_lat % GRID_W == 0
    qk = RET_HEADS * RET_DK

    xs = jnp.concatenate([x.reshape(n_lat_total, d), ctx.reshape(n_ctx_total, d)], axis=0).astype(F32)
    cc = jnp.zeros((MOD_GROUPS, d), F32).at[:nb].set(c.astype(F32)).at[nb].set(c_ctx.astype(F32))
    mod = _mod_table(cc, mod_w, mod_b).reshape(depth, MOD_GROUPS, 6, d)
    cos_t, sin_t = _rope_tables(nb, n_lat, n_ctx)
    ne = router_w.shape[2]
    wgu_bf = _prep_wgu(moe_w_gu.reshape((depth * ne,) + moe_w_gu.shape[2:]))
    wd_all = moe_w_down.reshape((depth * ne,) + moe_w_down.shape[2:])

    out = None
    for layer in range(depth):
        last = layer == depth - 1
        nrows = n_lat_total if last else nt
        mod_l = mod[layer]
        if layer % 2 == 0:
            e = layer // 2
            w_in = ev_w_in[e]
            w_in = jnp.concatenate([_deinterleave_heads(w_in[:, :qk]), _deinterleave_heads(w_in[:, qk:2 * qk]),
                                    w_in[:, 2 * qk:]], axis=1).astype(BF16)
            z = _premix_even(xs, mod_l, norm_mix_g[layer], w_in, cos_t, sin_t, nb, n_lat, n_ctx_total)
            lg = jnp.stack([-jnp.exp(ret_decay_f[e].astype(F32)), -jnp.exp(ret_decay_b[e].astype(F32))])
            gn = ret_gn_g[e].reshape(1, qk).astype(F32)
            s0 = jnp.zeros((nb, RET_HEADS, 2, RET_DK, RET_DK), F32)
            ret0 = jnp.zeros((nt, qk), BF16)
            ret, st = _retention(z, lg, gn, s0, ret0, nb, n_ctx, n_lat_total)
            ret, _ = _retention(z, lg, gn, st, ret, nb, n_lat, 0)
            cv = _conv_module(z, conv_w[e].astype(F32), conv_b[e].astype(F32), conv_ln_g[e].astype(F32),
                              conv_ln_b[e].astype(F32), nb, n_lat, n_ctx)
            xs = _outproj_even(ret, cv, ev_w_out[e].astype(BF16), xs, mod_l, nb, n_lat, n_ctx_total)
        else:
            j = layer // 2
            uv = _premix_odd(xs, mod_l, norm_mix_g[layer], od_w_in[j].astype(BF16), nb, n_lat, nrows)
            sw = od_w_out.shape[1]
            bias_full = jnp.repeat(sgu_b[j].astype(F32).T, sw // SGU_GROUPS, axis=1)
            xs = _sgu_out(uv, sgu_ln_g[j].astype(F32), sgu_ln_b[j].astype(F32), sgu_w[j].astype(BF16), bias_full,
                          od_w_out[j].astype(BF16), xs, mod_l, nb, n_lat, nrows)
        res = _moe(xs, mod_l, norm_ffn_g[layer], router_w[layer], router_b[layer], wgu_bf, moe_b_gu[layer],
                   wd_all, moe_b_down[layer], layer * ne, final_g.astype(F32), nb, n_lat, nrows, last)
        if last:
            out = res
        else:
            xs = res
    return out.reshape(nb, n_lat, d).astype(x.dtype)
```

```python
import functools
import math

import jax
import jax.numpy as jnp
from jax import lax
from jax.experimental import pallas as pl
from jax.experimental.pallas import tpu as pltpu

F32 = jnp.float32
BF16 = jnp.bfloat16

LANES = 128
CHUNK_ROWS = 16
RET_HEADS = 8
RET_DK = 128
RET_CHUNK = 128
ROPE_AXES = (32, 48, 48)
ROPE_THETA = 10000.0
GRID_W = 64
SGU_GROUPS = 8
SGU_CHUNK = 128
TOP_K = 4
MOE_TILE = 256
FETCH_UNROLL = 4
SWIGLU_LIMIT = 7.0
SWIGLU_ALPHA = 1.702
EPS = 1e-6
MOD_GROUPS = 8
HALO = 16
VMEM_CAP = 58 * 2 ** 20


def _cparams(sem, vmem_mb):
    return pltpu.CompilerParams(dimension_semantics=sem,
                                vmem_limit_bytes=min(int(vmem_mb * 2 ** 20), VMEM_CAP))


def _pick_tile(n_lat, n_ctx, cands):
    for t in cands:
        if n_lat % t == 0 and n_ctx % t == 0:
            return t
    raise ValueError("no token tile divides the latent and context lengths")


def _group_of(i, blocks_per_batch, nb):
    return jnp.minimum(i // blocks_per_batch, nb)


def _norm_mod(x, g, shift, scale):
    ms = jnp.mean(x * x, axis=-1, keepdims=True)
    return (x * lax.rsqrt(ms + EPS) * g) * (1.0 + scale) + shift


def _mod_kernel(c_ref, w_ref, b_ref, o_ref):
    c = c_ref[...]
    s = (c * jax.nn.sigmoid(c)).astype(BF16)
    o_ref[0] = jnp.dot(s, w_ref[0].astype(BF16), preferred_element_type=F32) + b_ref[0]


def _mod_table(cc, mod_w, mod_b):
    depth, d, n6 = mod_w.shape
    tn = min(d, 1024)
    return pl.pallas_call(
        _mod_kernel,
        grid=(depth, n6 // tn),
        in_specs=[pl.BlockSpec((MOD_GROUPS, d), lambda l, j: (0, 0)),
                  pl.BlockSpec((1, d, tn), lambda l, j: (l, 0, j)),
                  pl.BlockSpec((1, 1, tn), lambda l, j: (l, 0, j))],
        out_specs=pl.BlockSpec((1, MOD_GROUPS, tn), lambda l, j: (l, 0, j)),
        out_shape=jax.ShapeDtypeStruct((depth, MOD_GROUPS, n6), F32),
        compiler_params=_cparams(("arbitrary", "arbitrary"), 2 * d * tn * 4 / 2 ** 20 + 8),
        name="mod_table",
    )(cc, mod_w, mod_b.reshape(depth, 1, n6))


def _premix_even_kernel(x_ref, mod_ref, g_ref, w_ref, cos_ref, sin_ref, o_ref, h_scr, *, kscale):
    j = pl.program_id(1)

    @pl.when(j == 0)
    def _():
        m = mod_ref[0]
        h_scr[...] = _norm_mod(x_ref[...], g_ref[...], m[0:1], m[1:2]).astype(BF16)

    z = jnp.dot(h_scr[...], w_ref[...], preferred_element_type=F32)

    @pl.when(j < 2)
    def _():
        c = cos_ref[...]
        s = sin_ref[...]
        scale = jnp.where(j == 1, kscale, 1.0).astype(F32)
        for hh in range(z.shape[1] // RET_DK):
            t = z[:, hh * RET_DK:(hh + 1) * RET_DK]
            r = pltpu.roll(t, RET_DK // 2, axis=1)
            o_ref[:, hh * RET_DK:(hh + 1) * RET_DK] = ((t * c + r * s) * scale).astype(BF16)

    @pl.when(j >= 2)
    def _():
        o_ref[...] = z.astype(BF16)


def _premix_even(xs, mod_l, g, w_bf, cos_t, sin_t, nb, n_lat, n_ctx_total):
    nt, d = xs.shape
    n_out = w_bf.shape[1]
    tm = _pick_tile(n_lat, n_ctx_total, (1024, 512, 256))
    tn = RET_HEADS * RET_DK
    bpb = n_lat // tm
    n_lat_blocks = nb * bpb

    def tab_idx(i, j):
        return (jnp.where(i < n_lat_blocks, i % bpb, bpb + (i - n_lat_blocks)), 0)

    return pl.pallas_call(
        functools.partial(_premix_even_kernel, kscale=RET_DK ** -0.5),
        grid=(nt // tm, n_out // tn),
        in_specs=[pl.BlockSpec((tm, d), lambda i, j: (i, 0)),
                  pl.BlockSpec((1, 6, d), lambda i, j: (_group_of(i, bpb, nb), 0, 0)),
                  pl.BlockSpec((1, d), lambda i, j: (0, 0)),
                  pl.BlockSpec((d, tn), lambda i, j: (0, j)),
                  pl.BlockSpec((tm, RET_DK), tab_idx),
                  pl.BlockSpec((tm, RET_DK), tab_idx)],
        out_specs=pl.BlockSpec((tm, tn), lambda i, j: (i, j)),
        out_shape=jax.ShapeDtypeStruct((nt, n_out), BF16),
        scratch_shapes=[pltpu.VMEM((tm, d), BF16)],
        compiler_params=_cparams(("arbitrary", "arbitrary"),
                                 (2 * tm * d * 4 + tm * d * 2 + 2 * d * tn * 2 + 2 * tm * tn * 2
                                  + 2 * tm * tn * 4) / 2 ** 20 + 8),
        name="premix_even",
    )(xs, mod_l, g.reshape(1, d), w_bf, cos_t, sin_t)


def _retention_kernel(lg_ref, q_ref, k_ref, v_ref, g_ref, gn_ref, s0_ref, ret_in_ref, ret_ref, sfin_ref,
                      st_scr, *, nc, unroll):
    del ret_in_ref
    hh = pl.program_id(1)
    lgf = lg_ref[0, hh]
    lgb = lg_ref[1, hh]
    c = RET_CHUNK
    ri = lax.broadcasted_iota(jnp.int32, (c, c), 0).astype(F32)
    ci = lax.broadcasted_iota(jnp.int32, (c, c), 1).astype(F32)
    rel = ri - ci
    dmat = jnp.where(rel >= 0, jnp.exp(lgf * jnp.maximum(rel, 0.0)), jnp.exp(lgb * jnp.maximum(-rel, 0.0)))
    xi_f = jnp.exp(lgf * (ri + 1.0))
    xi_b = jnp.exp(lgb * (c - ri))
    zeta_f = jnp.exp(lgf * (c - 1.0 - ci))
    zeta_b = jnp.exp(lgb * ci)
    gc_f = jnp.exp(lgf * c + jnp.zeros((c, c), F32))
    gc_b = jnp.exp(lgb * c + jnp.zeros((c, c), F32))

    def local_sums(i, carry):
        for u in range(unroll):
            cc = i * unroll + u
            r0 = pl.multiple_of(cc * c, c)
            kt = k_ref[pl.ds(r0, c), :].astype(F32).T
            v = v_ref[pl.ds(r0, c), :]
            st_scr[cc, :, 0:c] = jnp.dot((kt * zeta_f).astype(BF16), v, preferred_element_type=F32)
            st_scr[cc, :, c:2 * c] = jnp.dot((kt * zeta_b).astype(BF16), v, preferred_element_type=F32)
        return carry

    lax.fori_loop(0, nc // unroll, local_sums, 0)

    def scans(i, carry):
        sf, sb = carry
        cb = nc - 1 - i
        uf = st_scr[i, :, 0:c]
        st_scr[i, :, 0:c] = sf
        ub = st_scr[cb, :, c:2 * c]
        st_scr[cb, :, c:2 * c] = sb
        return gc_f * sf + uf, gc_b * sb + ub

    sf, sb = lax.fori_loop(0, nc, scans, (s0_ref[0, 0, 0], s0_ref[0, 0, 1]))
    sfin_ref[0, 0, 0] = sf
    sfin_ref[0, 0, 1] = sb

    def outputs(i, carry):
        for u in range(unroll):
            cc = i * unroll + u
            r0 = pl.multiple_of(cc * c, c)
            q = q_ref[pl.ds(r0, c), :]
            a = lax.dot_general(q, k_ref[pl.ds(r0, c), :], (((1,), (1,)), ((), ())), preferred_element_type=F32)
            o = jnp.dot((a * dmat).astype(BF16), v_ref[pl.ds(r0, c), :], preferred_element_type=F32)
            cr = jnp.dot(q, st_scr[cc].astype(BF16), preferred_element_type=F32)
            o = o + cr[:, 0:c] * xi_f + cr[:, c:2 * c] * xi_b
            mu = jnp.mean(o, axis=-1, keepdims=True)
            dlt = o - mu
            var = jnp.mean(dlt * dlt, axis=-1, keepdims=True)
            y = dlt * lax.rsqrt(var + EPS) * gn_ref[...]
            g = g_ref[pl.ds(r0, c), :].astype(F32)
            ret_ref[pl.ds(r0, c), :] = (g * jax.nn.sigmoid(g) * y).astype(BF16)
        return carry

    lax.fori_loop(0, nc // unroll, outputs, 0)


def _retention(z, lg, gn, s0, ret_prev, nb, seq, row0):
    nt = z.shape[0]
    nc = seq // RET_CHUNK
    rb0 = row0 // seq
    hcols = RET_HEADS

    def col(off):
        return lambda b, h: (rb0 + b, off * hcols + h)

    ret, sfin = pl.pallas_call(
        functools.partial(_retention_kernel, nc=nc, unroll=max(u for u in (16, 8, 4, 2, 1) if nc % u == 0)),
        grid=(nb, RET_HEADS),
        in_specs=[pl.BlockSpec(memory_space=pltpu.SMEM),
                  pl.BlockSpec((seq, RET_DK), col(0)),
                  pl.BlockSpec((seq, RET_DK), col(1)),
                  pl.BlockSpec((seq, RET_DK), col(2)),
                  pl.BlockSpec((seq, RET_DK), col(3)),
                  pl.BlockSpec((1, RET_DK), lambda b, h: (0, h)),
                  pl.BlockSpec((1, 1, 2, RET_DK, RET_DK), lambda b, h: (b, h, 0, 0, 0)),
                  pl.BlockSpec(memory_space=pl.ANY)],
        out_specs=[pl.BlockSpec((seq, RET_DK), lambda b, h: (rb0 + b, h)),
                   pl.BlockSpec((1, 1, 2, RET_DK, RET_DK), lambda b, h: (b, h, 0, 0, 0))],
        out_shape=[jax.ShapeDtypeStruct((nt, RET_HEADS * RET_DK), BF16),
                   jax.ShapeDtypeStruct((nb, RET_HEADS, 2, RET_DK, RET_DK), F32)],
        scratch_shapes=[pltpu.VMEM((nc, RET_DK, 2 * RET_DK), F32)],
        input_output_aliases={7: 0},
        compiler_params=_cparams(("arbitrary", "arbitrary"),
                                 (10 * seq * RET_DK * 2 + 2 * nc * RET_DK * RET_DK * 4) / 2 ** 20 + 8),
        name="retention",
    )(lg, z, z, z, z, gn, s0, ret_prev)
    return ret, sfin


def _conv_kernel(a_ref, b_ref, ap_ref, bp_ref, an_ref, bn_ref, w_ref, cb_ref, lg_ref, lb_ref, o_ref, ext, shifted,
                 *, tm, n_lat_total, n_lat, n_ctx, ktaps):
    i = pl.program_id(0)
    row0 = i * tm
    is_lat = row0 < n_lat_total
    pos = jnp.where(is_lat, row0 % n_lat, (row0 - n_lat_total) % n_ctx)
    seq = jnp.where(is_lat, n_lat, n_ctx)
    keep_prev = (pos != 0).astype(F32)
    keep_next = (pos + tm != seq).astype(F32)

    def glu(a, b):
        return a.astype(F32) * jax.nn.sigmoid(b.astype(F32))

    ext[0:HALO, :] = glu(ap_ref[...], bp_ref[...]) * keep_prev
    ext[HALO:HALO + tm, :] = glu(a_ref[...], b_ref[...])
    ext[HALO + tm:HALO + tm + HALO, :] = glu(an_ref[...], bn_ref[...]) * keep_next

    half = ktaps // 2
    rc = 32
    sub = 8
    span = shifted.shape[1]
    for s in range(1, sub):
        shifted[s - 1] = ext[s:s + span, :]
    for r in range(tm // rc):
        acc = jnp.zeros((rc, a_ref.shape[1]), F32)
        for j in range(ktaps):
            o = HALO + r * rc + j - half
            s, base = o % sub, o - o % sub
            win = ext[base:base + rc, :] if s == 0 else shifted[s - 1, base:base + rc, :]
            acc = acc + w_ref[j:j + 1, :] * win
        y = acc + cb_ref[...]
        mu = jnp.mean(y, axis=-1, keepdims=True)
        dlt = y - mu
        var = jnp.mean(dlt * dlt, axis=-1, keepdims=True)
        yn = dlt * lax.rsqrt(var + EPS) * lg_ref[...] + lb_ref[...]
        o_ref[r * rc:(r + 1) * rc, :] = (yn * jax.nn.sigmoid(yn)).astype(BF16)


def _conv_module(z, conv_w, conv_b, ln_g, ln_b, nb, n_lat, n_ctx):
    nt = z.shape[0]
    ktaps, cw = conv_w.shape
    tm = 256
    assert n_lat % tm == 0 and n_ctx % tm == 0 and ktaps // 2 <= HALO
    a_blk = (2 * RET_HEADS * RET_DK + 2 * RET_HEADS * RET_DK) // cw
    hb = tm // HALO
    last = nt // HALO - 1
    kern = functools.partial(_conv_kernel, tm=tm, n_lat_total=nb * n_lat, n_lat=n_lat, n_ctx=n_ctx, ktaps=ktaps)
    vec = lambda i: (0, 0)
    return pl.pallas_call(
        kern,
        grid=(nt // tm,),
        in_specs=[pl.BlockSpec((tm, cw), lambda i: (i, a_blk)),
                  pl.BlockSpec((tm, cw), lambda i: (i, a_blk + 1)),
                  pl.BlockSpec((HALO, cw), lambda i: (jnp.maximum(i * hb - 1, 0), a_blk)),
                  pl.BlockSpec((HALO, cw), lambda i: (jnp.maximum(i * hb - 1, 0), a_blk + 1)),
                  pl.BlockSpec((HALO, cw), lambda i: (jnp.minimum((i + 1) * hb, last), a_blk)),
                  pl.BlockSpec((HALO, cw), lambda i: (jnp.minimum((i + 1) * hb, last), a_blk + 1)),
                  pl.BlockSpec((ktaps, cw), vec),
                  pl.BlockSpec((1, cw), vec),
                  pl.BlockSpec((1, cw), vec),
                  pl.BlockSpec((1, cw), vec)],
        out_specs=pl.BlockSpec((tm, cw), lambda i: (i, 0)),
        out_shape=jax.ShapeDtypeStruct((nt, cw), BF16),
        scratch_shapes=[pltpu.VMEM((tm + 2 * HALO, cw), F32),
                        pltpu.VMEM((7, tm + 2 * HALO - 8, cw), F32)],
        compiler_params=_cparams(("arbitrary",), 40),
        name="conv_module",
    )(z, z, z, z, z, z, conv_w, conv_b.reshape(1, cw), ln_g.reshape(1, cw), ln_b.reshape(1, cw))


def _outproj_even_kernel(ret_ref, cv_ref, w_ref, x_ref, mod_ref, o_ref):
    rw = ret_ref.shape[1]
    y = jnp.dot(ret_ref[...], w_ref[0:rw, :], preferred_element_type=F32)
    y = y + jnp.dot(cv_ref[...], w_ref[rw:, :], preferred_element_type=F32)
    o_ref[...] = x_ref[...] + mod_ref[0][2:3] * y


def _outproj_even(ret, cv, w_bf, xs, mod_l, nb, n_lat, n_ctx_total):
    nt, d = xs.shape
    tm = _pick_tile(n_lat, n_ctx_total, (512, 256))
    bpb = n_lat // tm
    rw, cw = ret.shape[1], cv.shape[1]
    return pl.pallas_call(
        _outproj_even_kernel,
        grid=(nt // tm,),
        in_specs=[pl.BlockSpec((tm, rw), lambda i: (i, 0)),
                  pl.BlockSpec((tm, cw), lambda i: (i, 0)),
                  pl.BlockSpec((rw + cw, d), lambda i: (0, 0)),
                  pl.BlockSpec((tm, d), lambda i: (i, 0)),
                  pl.BlockSpec((1, 6, d), lambda i: (_group_of(i, bpb, nb), 0, 0))],
        out_specs=pl.BlockSpec((tm, d), lambda i: (i, 0)),
        out_shape=jax.ShapeDtypeStruct((nt, d), F32),
        input_output_aliases={3: 0},
        compiler_params=_cparams(("arbitrary",),
                                 (2 * (rw + cw) * d * 2 + 4 * tm * d * 4 + 2 * tm * (rw + cw) * 2
                                  + tm * d * 4) / 2 ** 20 + 8),
        name="outproj_even",
    )(ret, cv, w_bf, xs, mod_l)


def _premix_odd_kernel(x_ref, mod_ref, g_ref, w_ref, o_ref, h_scr):
    j = pl.program_id(1)

    @pl.when(j == 0)
    def _():
        m = mod_ref[0]
        h_scr[...] = _norm_mod(x_ref[...], g_ref[...], m[0:1], m[1:2]).astype(BF16)

    z = jnp.dot(h_scr[...], w_ref[...], preferred_element_type=F32)
    o_ref[...] = (0.5 * z * (1.0 + lax.erf(z * (2.0 ** -0.5)))).astype(BF16)


def _premix_odd(xs, mod_l, g, w_bf, nb, n_lat, nrows):
    d = xs.shape[1]
    n_out = w_bf.shape[1]
    tm = _pick_tile(n_lat, nrows, (1024, 512, 256))
    tn = 1024
    bpb = n_lat // tm
    return pl.pallas_call(
        _premix_odd_kernel,
        grid=(nrows // tm, n_out // tn),
        in_specs=[pl.BlockSpec((tm, d), lambda i, j: (i, 0)),
                  pl.BlockSpec((1, 6, d), lambda i, j: (_group_of(i, bpb, nb), 0, 0)),
                  pl.BlockSpec((1, d), lambda i, j: (0, 0)),
                  pl.BlockSpec((d, tn), lambda i, j: (0, j))],
        out_specs=pl.BlockSpec((tm, tn), lambda i, j: (i, j)),
        out_shape=jax.ShapeDtypeStruct((nrows, n_out), BF16),
        scratch_shapes=[pltpu.VMEM((tm, d), BF16)],
        compiler_params=_cparams(("arbitrary", "arbitrary"),
                                 (2 * tm * d * 4 + tm * d * 2 + 2 * d * tn * 2 + 2 * tm * tn * 2
                                  + 2 * tm * tn * 4) / 2 ** 20 + 8),
        name="premix_odd",
    )(xs, mod_l, g.reshape(1, d), w_bf)


def _sgu_out_kernel(u_ref, v_ref, lg_ref, lb_ref, ws_ref, bs_ref, w_ref, x_ref, mod_ref, o_ref, p_scr):
    tm, sw = u_ref.shape
    gw = sw // SGU_GROUPS
    v = v_ref[...].astype(F32)
    mu = jnp.mean(v, axis=-1, keepdims=True)
    dlt = v - mu
    var = jnp.mean(dlt * dlt, axis=-1, keepdims=True)
    p_scr[...] = (dlt * lax.rsqrt(var + EPS) * lg_ref[...] + lb_ref[...]).astype(BF16)
    for cc in range(tm // SGU_CHUNK):
        rows = slice(cc * SGU_CHUNK, (cc + 1) * SGU_CHUNK)
        for gi in range(SGU_GROUPS):
            cols = slice(gi * gw, (gi + 1) * gw)
            mixed = jnp.dot(ws_ref[gi], p_scr[rows, cols], preferred_element_type=F32) + bs_ref[:, cols]
            p_scr[rows, cols] = (u_ref[rows, cols].astype(F32) * mixed).astype(BF16)
    y = jnp.dot(p_scr[...], w_ref[...], preferred_element_type=F32)
    o_ref[...] = x_ref[...] + mod_ref[0][2:3] * y


def _sgu_out(uv, ln_g, ln_b, ws_bf, bias_full, w_bf, xs, mod_l, nb, n_lat, nrows):
    nt, d = xs.shape
    sw = w_bf.shape[0]
    tm = _pick_tile(n_lat, nrows, (512, 256))
    bpb = n_lat // tm
    vec = lambda i: (0, 0)
    return pl.pallas_call(
        _sgu_out_kernel,
        grid=(nrows // tm,),
        in_specs=[pl.BlockSpec((tm, sw), lambda i: (i, 0)),
                  pl.BlockSpec((tm, sw), lambda i: (i, 1)),
                  pl.BlockSpec((1, sw), vec),
                  pl.BlockSpec((1, sw), vec),
                  pl.BlockSpec((SGU_GROUPS, SGU_CHUNK, SGU_CHUNK), lambda i: (0, 0, 0)),
                  pl.BlockSpec((SGU_CHUNK, sw), vec),
                  pl.BlockSpec((sw, d), vec),
                  pl.BlockSpec((tm, d), lambda i: (i, 0)),
                  pl.BlockSpec((1, 6, d), lambda i: (_group_of(i, bpb, nb), 0, 0))],
        out_specs=pl.BlockSpec((tm, d), lambda i: (i, 0)),
        out_shape=jax.ShapeDtypeStruct((nt, d), F32),
        scratch_shapes=[pltpu.VMEM((tm, sw), BF16)],
        input_output_aliases={7: 0},
        compiler_params=_cparams(("arbitrary",),
                                 (2 * sw * d * 2 + 4 * tm * d * 4 + 4 * tm * sw * 2 + tm * sw * 2
                                  + 2 * SGU_CHUNK * sw * 4 + 2 * tm * sw * 4) / 2 ** 20 + 8),
        name="sgu_out",
    )(uv, uv, ln_g.reshape(1, sw), ln_b.reshape(1, sw), ws_bf, bias_full, w_bf, xs, mod_l)


def _router_kernel(x_ref, mod_ref, g_ref, w_ref, rb_ref, h_ref, aux_ref, cnt_ref, before_ref, carry):
    i = pl.program_id(0)
    tm = x_ref.shape[0]

    @pl.when(i == 0)
    def _():
        carry[...] = jnp.zeros_like(carry)

    m = mod_ref[0]
    h = _norm_mod(x_ref[...], g_ref[...], m[3:4], m[4:5])
    kh = h.shape[1] // 2
    lo_bits = pltpu.bitcast(h[:, :kh].astype(BF16).astype(F32), jnp.uint32)
    hi_bits = pltpu.bitcast(h[:, kh:].astype(BF16).astype(F32), jnp.uint32)
    h_ref[...] = (lo_bits >> 16) | (hi_bits & jnp.uint32(0xFFFF0000))
    h_hi = h.astype(BF16)
    h_lo = (h - h_hi.astype(F32)).astype(BF16)
    p_hi = jnp.dot(h_hi, w_ref[...], preferred_element_type=F32)
    p_lo = jnp.dot(h_lo, w_ref[:, 0:LANES], preferred_element_type=F32)
    logits = p_hi[:, 0:LANES] + p_lo + p_hi[:, LANES:2 * LANES] + rb_ref[...]

    lane = lax.broadcasted_iota(jnp.int32, (tm, LANES), 1).astype(F32)
    vals, idxs = [], []
    l = logits
    for _ in range(TOP_K):
        mx = jnp.max(l, axis=-1, keepdims=True)
        ix = jnp.min(jnp.where(l == mx, lane, float(LANES)), axis=-1, keepdims=True)
        vals.append(mx)
        idxs.append(ix)
        l = jnp.where(lane == ix, -jnp.inf, l)
    es = [jnp.exp(v - vals[0]) for v in vals]
    den = es[0]
    for e in es[1:]:
        den = den + e

    sel = jnp.zeros((tm, LANES), F32)
    for ix in idxs:
        sel = sel + jnp.where(lane == ix, 1.0, 0.0)
    rr = lax.broadcasted_iota(jnp.int32, (tm, tm), 0)
    rc = lax.broadcasted_iota(jnp.int32, (tm, tm), 1)
    ltri = jnp.where(rc < rr, 1.0, 0.0).astype(BF16)
    rank = jnp.dot(ltri, sel.astype(BF16), preferred_element_type=F32) + carry[...]
    before_ref[0] = jnp.broadcast_to(carry[...], before_ref.shape[1:])
    carry[...] = carry[...] + jnp.sum(sel, axis=0, keepdims=True)
    cnt_ref[...] = carry[...]

    aux = jnp.zeros((tm, LANES), F32)
    for r in range(TOP_K):
        rk = jnp.sum(jnp.where(lane == idxs[r], rank, 0.0), axis=-1, keepdims=True)
        aux = jnp.where(lane == float(r), idxs[r], aux)
        aux = jnp.where(lane == float(TOP_K + r), rk, aux)
        aux = jnp.where(lane == float(2 * TOP_K + r), es[r] / den, aux)
    aux_ref[...] = aux


def _router(xs, mod_l, g, w_hilo, rb, nb, n_lat, nrows):
    d = xs.shape[1]
    tm = MOE_TILE
    bpb = n_lat // tm
    vec = lambda i: (0, 0)
    return pl.pallas_call(
        _router_kernel,
        grid=(nrows // tm,),
        in_specs=[pl.BlockSpec((tm, d), lambda i: (i, 0)),
                  pl.BlockSpec((1, 6, d), lambda i: (_group_of(i, bpb, nb), 0, 0)),
                  pl.BlockSpec((1, d), vec),
                  pl.BlockSpec((d, 2 * LANES), vec),
                  pl.BlockSpec((1, LANES), vec)],
        out_specs=[pl.BlockSpec((tm, d // 2), lambda i: (i, 0)),
                   pl.BlockSpec((tm, LANES), lambda i: (i, 0)),
                   pl.BlockSpec((1, LANES), vec),
                   pl.BlockSpec((1, 8, LANES), lambda i: (i, 0, 0))],
        out_shape=[jax.ShapeDtypeStruct((nrows, d // 2), jnp.uint32),
                   jax.ShapeDtypeStruct((nrows, LANES), F32),
                   jax.ShapeDtypeStruct((1, LANES), F32),
                   jax.ShapeDtypeStruct((nrows // tm, 8, LANES), F32)],
        scratch_shapes=[pltpu.VMEM((1, LANES), F32)],
        compiler_params=_cparams(("arbitrary",), 32),
        name="router",
    )(xs, mod_l, g.reshape(1, d), w_hilo, rb)


def _dispatch_kernel(pos_ref, h_ref, hs_ref, sem):
    tm = h_ref.shape[0]

    def row_copy(t, p):
        return pltpu.make_async_copy(h_ref.at[pl.ds(t, 1), :], hs_ref.at[pl.ds(p, 1), :], sem)

    def issue(t, carry):
        for r in range(TOP_K):
            row_copy(t, pos_ref[0, 0, t * TOP_K + r]).start()
        return carry

    lax.fori_loop(0, tm, issue, 0)
    for _ in range(TOP_K):
        pltpu.make_async_copy(h_ref, hs_ref.at[pl.ds(0, tm), :], sem).wait()


def _dispatch(h, pos):
    nrows, d = h.shape
    tm = MOE_TILE
    pos3 = pos.reshape(nrows // tm, 1, tm * TOP_K)
    return pl.pallas_call(
        _dispatch_kernel,
        grid=(nrows // tm,),
        in_specs=[pl.BlockSpec((1, 1, tm * TOP_K), lambda i: (i, 0, 0), memory_space=pltpu.SMEM),
                  pl.BlockSpec((tm, d), lambda i: (i, 0))],
        out_specs=pl.BlockSpec(memory_space=pl.ANY),
        out_shape=jax.ShapeDtypeStruct((nrows * TOP_K, d), h.dtype),
        scratch_shapes=[pltpu.SemaphoreType.DMA(())],
        compiler_params=_cparams(("arbitrary",), 16),
        name="moe_dispatch",
    )(pos3, h)


def _pair_perm():
    i = jnp.arange(2 * LANES)[:, None]
    j = jnp.arange(2 * LANES)[None, :]
    return jnp.where(j < LANES, i == 2 * j, i == 2 * (j - LANES) + 1).astype(BF16)


def _prep_wgu_kernel(w_ref, p_ref, o_ref):
    blk = 2 * LANES
    for b in range(w_ref.shape[2] // blk):
        cols = slice(b * blk, (b + 1) * blk)
        o_ref[0, :, cols] = jnp.dot(w_ref[0, :, cols].astype(BF16), p_ref[...],
                                    preferred_element_type=F32).astype(BF16)


def _prep_wgu(wgu):
    lead = wgu.shape[:-2]
    d, n2 = wgu.shape[-2:]
    ne = math.prod(lead)
    tr = 1024 if d % 1024 == 0 else d
    out = pl.pallas_call(
        _prep_wgu_kernel,
        grid=(ne, d // tr),
        in_specs=[pl.BlockSpec((1, tr, n2), lambda e, r: (e, r, 0)),
                  pl.BlockSpec((2 * LANES, 2 * LANES), lambda e, r: (0, 0))],
        out_specs=pl.BlockSpec((1, tr, n2), lambda e, r: (e, r, 0)),
        out_shape=jax.ShapeDtypeStruct((ne, d, n2), BF16),
        compiler_params=_cparams(("arbitrary", "arbitrary"), 3 * tr * n2 * 4 / 2 ** 20 + 8),
        name="prep_wgu",
    )(wgu.reshape(ne, d, n2), _pair_perm())
    return out.reshape(lead + (d, n2))


def _expert_kernel(tile_ref, exp_ref, lo_ref, hi_ref, first_ref, hs_ref, wgu_ref, bgu_ref, wd_ref, bd_ref, o_ref):
    del tile_ref, exp_ref
    w = pl.program_id(0)
    tm = hs_ref.shape[0]
    lo, hi, first = lo_ref[w], hi_ref[w], first_ref[w]

    def rows_block(r0, nr):
        rows = slice(r0, r0 + nr)
        words = hs_ref[rows, :]
        kh = words.shape[1]
        x_lo = pltpu.bitcast(words << 16, F32).astype(BF16)
        x_hi = pltpu.bitcast(words & jnp.uint32(0xFFFF0000), F32).astype(BF16)
        z = (jnp.dot(x_lo, wgu_ref[0, 0:kh, :], preferred_element_type=F32)
             + jnp.dot(x_hi, wgu_ref[0, kh:, :], preferred_element_type=F32)) + bgu_ref[0]
        parts = []
        for b in range(z.shape[1] // (2 * LANES)):
            zg = jnp.minimum(z[:, 2 * b * LANES:(2 * b + 1) * LANES], SWIGLU_LIMIT)
            zu = jnp.clip(z[:, (2 * b + 1) * LANES:(2 * b + 2) * LANES], -SWIGLU_LIMIT, SWIGLU_LIMIT)
            parts.append(((zu + 1.0) * (zg * jax.nn.sigmoid(SWIGLU_ALPHA * zg))).astype(BF16))
        act = jnp.concatenate(parts, axis=1)
        y = jnp.dot(act, wd_ref[0].astype(BF16), preferred_element_type=F32) + bd_ref[0]
        y = y.astype(o_ref.dtype)
        row = r0 + lax.broadcasted_iota(jnp.int32, (nr, 1), 0)
        mine = (row >= lo) & (row < hi)

        @pl.when(first == 1)
        def _():
            o_ref[rows, :] = jnp.where(mine, y, jnp.zeros_like(y))

        @pl.when(first == 0)
        def _():
            o_ref[rows, :] = jnp.where(mine, y, o_ref[rows, :])

    whole = (lo == 0) & (hi == tm)

    @pl.when(whole)
    def _():
        rows_block(0, tm)

    @pl.when(jnp.logical_not(whole))
    def _():
        half = tm // 2
        for r0 in (0, half):
            touched = (lo < r0 + half) & (hi > r0)

            @pl.when(touched)
            def _():
                rows_block(r0, half)

            @pl.when(jnp.logical_not(touched) & (first == 1))
            def _():
                o_ref[r0:r0 + half, :] = jnp.zeros((half, o_ref.shape[1]), o_ref.dtype)


def _experts(hs, items, wgu_bf, bgu, wd_all, bd, tm, ebase):
    p, dw = hs.shape
    d, n2 = wgu_bf.shape[1:]
    ne = bgu.shape[0]
    de = n2 // 2
    nw = items[0].shape[0]
    grid_spec = pltpu.PrefetchScalarGridSpec(
        num_scalar_prefetch=5,
        grid=(nw,),
        in_specs=[pl.BlockSpec((tm, dw), lambda w, t, e, lo, hi, f: (t[w], 0)),
                  pl.BlockSpec((1, d, n2), lambda w, t, e, lo, hi, f: (ebase + e[w], 0, 0)),
                  pl.BlockSpec((1, 1, n2), lambda w, t, e, lo, hi, f: (e[w], 0, 0)),
                  pl.BlockSpec((1, de, d), lambda w, t, e, lo, hi, f: (ebase + e[w], 0, 0)),
                  pl.BlockSpec((1, 1, d), lambda w, t, e, lo, hi, f: (e[w], 0, 0))],
        out_specs=pl.BlockSpec((tm, d), lambda w, t, e, lo, hi, f: (t[w], 0)),
    )
    return pl.pallas_call(
        _expert_kernel,
        grid_spec=grid_spec,
        out_shape=jax.ShapeDtypeStruct((p, d), BF16),
        compiler_params=_cparams(("arbitrary",),
                                 (2 * tm * d * 4 + 2 * tm * d * 2 + 2 * d * n2 * 2 + 2 * de * d * 4 + de * d * 2
                                  + 2 * tm * n2 * 4 + tm * d * 4) / 2 ** 20 + 6),
        name="moe_experts",
    )(*items, hs, wgu_bf, bgu.reshape(ne, 1, n2), wd_all, bd.reshape(ne, 1, d))


def _expert_items(counts, p, tm, ne):
    ntiles = p // tm
    nw = ntiles + ne - 1
    off = jnp.concatenate([jnp.zeros((1,), jnp.int32), jnp.cumsum(counts)])
    first_tile = off[:-1] // tm
    last_tile = (off[1:] - 1) // tm
    ntile_e = jnp.where(counts > 0, last_tile - first_tile + 1, 0)
    base_incl = jnp.cumsum(ntile_e)
    base = base_incl - ntile_e
    total = base_incl[-1]
    w = jnp.arange(nw, dtype=jnp.int32)
    wc = jnp.minimum(w, total - 1)
    e = jnp.sum((base_incl[None, :] <= wc[:, None]).astype(jnp.int32), axis=1)
    t = first_tile[e] + (wc - base[e])
    lo = jnp.clip(off[e] - t * tm, 0, tm)
    hi = jnp.clip(off[e + 1] - t * tm, 0, tm)
    valid = w < total
    lo = jnp.where(valid, lo, 0)
    hi = jnp.where(valid, hi, 0)
    prev_t = jnp.concatenate([jnp.full((1,), -1, jnp.int32), t[:-1]])
    first = (valid & (t != prev_t)).astype(jnp.int32)
    return (t.astype(jnp.int32), e, lo.astype(jnp.int32), hi.astype(jnp.int32), first), off


def _combine_kernel(src_ref, tot_ref, ys_ref, aux_ref, tab_ref, x_ref, mod_ref, fg_ref, o_ref,
                    buf, sem, *, final, ntiles):
    i = pl.program_id(0)
    tm = x_ref.shape[0]
    nrow = buf.shape[1]
    slot = i % 2

    def fetch(tile, sl):
        def per_group(gi, c):
            for u in range(FETCH_UNROLL):
                s = gi * FETCH_UNROLL + u
                src_row = pl.multiple_of(src_ref[tile * (nrow // CHUNK_ROWS) + s] * CHUNK_ROWS, CHUNK_ROWS)
                dst_row = pl.multiple_of(s * CHUNK_ROWS, CHUNK_ROWS)
                pltpu.make_async_copy(ys_ref.at[pl.ds(src_row, CHUNK_ROWS), :],
                                      buf.at[sl, pl.ds(dst_row, CHUNK_ROWS), :], sem.at[sl]).start()
            return c
        lax.fori_loop(0, tot_ref[tile] // FETCH_UNROLL, per_group, 0)

    def wait_fetched(tile, sl):
        total = tot_ref[tile]
        for b in range((nrow // CHUNK_ROWS).bit_length()):
            rows = CHUNK_ROWS << b

            @pl.when((total >> b) & 1 == 1)
            def _():
                pltpu.make_async_copy(ys_ref.at[pl.ds(0, rows), :], buf.at[sl, pl.ds(0, rows), :],
                                      sem.at[sl]).wait()

    @pl.when(i == 0)
    def _():
        buf[...] = jnp.zeros_like(buf)
        fetch(0, 0)

    @pl.when(i + 1 < ntiles)
    def _():
        fetch(i + 1, 1 - slot)

    wait_fetched(i, slot)

    aux = aux_ref[...]
    tab = tab_ref[0]
    lane = lax.broadcasted_iota(jnp.int32, (tm, LANES), 1).astype(F32)
    col = lax.broadcasted_iota(jnp.int32, (tm, nrow), 1).astype(F32)
    gmat = jnp.zeros((tm, nrow), F32)
    for r in range(TOP_K):
        base = jnp.sum(jnp.where(lane == aux[:, r:r + 1], tab, 0.0), axis=-1, keepdims=True)
        lrow = base + aux[:, TOP_K + r:TOP_K + r + 1]
        gmat = gmat + jnp.where(col == lrow, aux[:, 2 * TOP_K + r:2 * TOP_K + r + 1], 0.0)
    y = jnp.dot(gmat.astype(BF16), buf[slot], preferred_element_type=F32)
    xn = x_ref[...] + mod_ref[0][5:6] * y
    if final:
        ms = jnp.mean(xn * xn, axis=-1, keepdims=True)
        xn = xn * lax.rsqrt(ms + EPS) * fg_ref[...]
    o_ref[...] = xn


def _combine_plan(off, before, counts, ne, max_chunks):
    start = off[None, :ne] + before
    end = off[None, :ne] + jnp.concatenate([before[1:], counts[None, :]], axis=0)
    c0 = start // CHUNK_ROWS
    cc = jnp.where(end > start, (end + CHUNK_ROWS - 1) // CHUNK_ROWS - c0, 0)
    sb_end = jnp.cumsum(cc, axis=1)
    sb = sb_end - cc
    tab = (sb - c0) * CHUNK_ROWS + off[None, :ne]
    tab = jnp.zeros((before.shape[0], 1, LANES), F32).at[:, 0, :ne].set(tab.astype(F32))
    slot = jnp.arange(max_chunks, dtype=jnp.int32)[None, :, None]
    owner = (sb[:, None, :] <= slot) & (slot < sb_end[:, None, :])
    src = jnp.sum(jnp.where(owner, (c0 - sb)[:, None, :], 0), axis=-1) + slot[:, :, 0]
    flat = lambda a: a.reshape(-1).astype(jnp.int32)
    tot = (sb_end[:, -1] + FETCH_UNROLL - 1) // FETCH_UNROLL * FETCH_UNROLL
    return flat(src), flat(tot), tab


def _combine(ys, plan, aux, xs, mod_l, final_g, nb, n_lat, nrows, final):
    d = xs.shape[1]
    tm = MOE_TILE
    bpb = n_lat // tm
    ntiles = nrows // tm
    src, tot, tab = plan
    max_chunks = src.shape[0] // ntiles
    out_rows = nrows if final else xs.shape[0]
    grid_spec = pltpu.PrefetchScalarGridSpec(
        num_scalar_prefetch=2,
        grid=(ntiles,),
        in_specs=[pl.BlockSpec(memory_space=pl.ANY),
                  pl.BlockSpec((tm, LANES), lambda i, *_: (i, 0)),
                  pl.BlockSpec((1, 1, LANES), lambda i, *_: (i, 0, 0)),
                  pl.BlockSpec((tm, d), lambda i, *_: (i, 0)),
                  pl.BlockSpec((1, 6, d), lambda i, *_: (_group_of(i, bpb, nb), 0, 0)),
                  pl.BlockSpec((1, d), lambda i, *_: (0, 0))],
        out_specs=pl.BlockSpec((tm, d), lambda i, *_: (i, 0)),
        scratch_shapes=[pltpu.VMEM((2, max_chunks * CHUNK_ROWS, d), ys.dtype), pltpu.SemaphoreType.DMA((2,))],
    )
    return pl.pallas_call(
        functools.partial(_combine_kernel, final=final, ntiles=ntiles),
        grid_spec=grid_spec,
        out_shape=jax.ShapeDtypeStruct((out_rows, d), F32),
        input_output_aliases={} if final else {5: 0},
        compiler_params=_cparams(("arbitrary",),
                                 (2 * max_chunks * CHUNK_ROWS * d * 2 + 4 * tm * max_chunks * CHUNK_ROWS * 4 + 8 * tm * d * 4) / 2 ** 20 + 6),
        name="moe_combine",
    )(src, tot, ys, aux, tab, xs, mod_l, final_g.reshape(1, d))


def _moe(xs, mod_l, g, rw, rb, wgu_bf, bgu, wd_all, bd, ebase, final_g, nb, n_lat, nrows, final):
    d = xs.shape[1]
    ne = rw.shape[1]
    rw_pad = jnp.zeros((d, LANES), F32).at[:, :ne].set(rw.astype(F32))
    whi = rw_pad.astype(BF16)
    wlo = (rw_pad - whi.astype(F32)).astype(BF16)
    rb_pad = jnp.full((1, LANES), -1e30, F32).at[0, :ne].set(rb.astype(F32))
    h, aux, cnt, before = _router(xs, mod_l, g, jnp.concatenate([whi, wlo], axis=1), rb_pad, nb, n_lat, nrows)

    counts = cnt[0, :ne].astype(jnp.int32)
    before = before[:, 0, :ne].astype(jnp.int32)
    tm_e = 512
    p = nrows * TOP_K
    items, off = _expert_items(counts, p, tm_e, ne)
    idx = aux[:, 0:TOP_K].astype(jnp.int32)
    rank = aux[:, TOP_K:2 * TOP_K].astype(jnp.int32)
    pos = (off[idx] + rank).reshape(-1)

    hs = _dispatch(h, pos)
    n2 = bgu.shape[1]
    bgu_p = bgu.astype(F32).reshape(ne, n2 // (2 * LANES), LANES, 2).transpose(0, 1, 3, 2).reshape(ne, n2)
    ys = _experts(hs, items, wgu_bf, bgu_p, wd_all, bd.astype(F32), tm_e, ebase)
    max_chunks = MOE_TILE * TOP_K // CHUNK_ROWS + 2 * ne
    assert max_chunks % FETCH_UNROLL == 0
    plan = _combine_plan(off, before, counts, ne, max_chunks)
    return _combine(ys, plan, aux, xs, mod_l, final_g, nb, n_lat, nrows, final)


def _rope_tables(nb, n_lat, n_ctx):
    def angles(seq_pos, row_pos, col_pos):
        parts = []
        for posv, dim in zip((seq_pos, row_pos, col_pos), ROPE_AXES):
            inv = ROPE_THETA ** (-jnp.arange(0, dim, 2, dtype=F32) / dim)
            parts.append(posv.astype(F32)[:, None] * inv[None, :])
        return jnp.concatenate(parts, axis=-1)

    rows = n_lat // GRID_W
    lat = angles(jnp.full((n_lat,), n_ctx, jnp.int32),
                 jnp.repeat(jnp.arange(rows, dtype=jnp.int32), GRID_W),
                 jnp.tile(jnp.arange(GRID_W, dtype=jnp.int32), rows))
    zl = jnp.zeros((n_ctx,), jnp.int32)
    ctx = angles(jnp.arange(n_ctx, dtype=jnp.int32), zl, zl)
    ang = jnp.concatenate([lat, jnp.tile(ctx, (nb, 1))], axis=0)
    cos, sin = jnp.cos(ang), jnp.sin(ang)
    return jnp.concatenate([cos, cos], axis=-1), jnp.concatenate([-sin, sin], axis=-1)


def _deinterleave_heads(w):
    d, n = w.shape
    w4 = w.reshape(d, n // RET_DK, RET_DK // 2, 2)
    return jnp.concatenate([w4[..., 0], w4[..., 1]], axis=-1).reshape(d, n)


def kernel(x, c, ctx, c_ctx, mod_w, mod_b, norm_mix_g, norm_ffn_g, ev_w_in, ev_w_out, ret_decay_f, ret_decay_b,
           ret_gn_g, conv_w, conv_b, conv_ln_g, conv_ln_b, od_w_in, od_w_out, sgu_ln_g, sgu_ln_b, sgu_w, sgu_b,
           router_w, router_b, moe_w_gu, moe_b_gu, moe_w_down, moe_b_down, final_g):
    nb, n_lat, d = x.shape
    n_ctx = ctx.shape[1]
    depth = mod_w.shape[0]
    n_lat_total = nb * n_lat
    n_ctx_total = nb * n_ctx
    nt = n_lat_total + n_ctx_total
    assert nb < MOD_GROUPS and n_lat % GRID_W == 0
    qk = RET_HEADS * RET_DK

    xs = jnp.concatenate([x.reshape(n_lat_total, d), ctx.reshape(n_ctx_total, d)], axis=0).astype(F32)
    cc = jnp.zeros((MOD_GROUPS, d), F32).at[:nb].set(c.astype(F32)).at[nb].set(c_ctx.astype(F32))
    mod = _mod_table(cc, mod_w, mod_b).reshape(depth, MOD_GROUPS, 6, d)
    cos_t, sin_t = _rope_tables(nb, n_lat, n_ctx)
    ne = router_w.shape[2]
    wgu_bf = _prep_wgu(moe_w_gu.reshape((depth * ne,) + moe_w_gu.shape[2:]))
    wd_all = moe_w_down.reshape((depth * ne,) + moe_w_down.shape[2:])

    out = None
    for layer in range(depth):
        last = layer == depth - 1
        nrows = n_lat_total if last else nt
        mod_l = mod[layer]
        if layer % 2 == 0:
            e = layer // 2
            w_in = ev_w_in[e]
            w_in = jnp.concatenate([_deinterleave_heads(w_in[:, :qk]), _deinterleave_heads(w_in[:, qk:2 * qk]),
                                    w_in[:, 2 * qk:]], axis=1).astype(BF16)
            z = _premix_even(xs, mod_l, norm_mix_g[layer], w_in, cos_t, sin_t, nb, n_lat, n_ctx_total)
            lg = jnp.stack([-jnp.exp(ret_decay_f[e].astype(F32)), -jnp.exp(ret_decay_b[e].astype(F32))])
            gn = ret_gn_g[e].reshape(1, qk).astype(F32)
            s0 = jnp.zeros((nb, RET_HEADS, 2, RET_DK, RET_DK), F32)
            ret0 = jnp.zeros((nt, qk), BF16)
            ret, st = _retention(z, lg, gn, s0, ret0, nb, n_ctx, n_lat_total)
            ret, _ = _retention(z, lg, gn, st, ret, nb, n_lat, 0)
            cv = _conv_module(z, conv_w[e].astype(F32), conv_b[e].astype(F32), conv_ln_g[e].astype(F32),
                              conv_ln_b[e].astype(F32), nb, n_lat, n_ctx)
            xs = _outproj_even(ret, cv, ev_w_out[e].astype(BF16), xs, mod_l, nb, n_lat, n_ctx_total)
        else:
            j = layer // 2
            uv = _premix_odd(xs, mod_l, norm_mix_g[layer], od_w_in[j].astype(BF16), nb, n_lat, nrows)
            sw = od_w_out.shape[1]
            bias_full = jnp.repeat(sgu_b[j].astype(F32).T, sw // SGU_GROUPS, axis=1)
            xs = _sgu_out(uv, sgu_ln_g[j].astype(F32), sgu_ln_b[j].astype(F32), sgu_w[j].astype(BF16), bias_full,
                          od_w_out[j].astype(BF16), xs, mod_l, nb, n_lat, nrows)
        res = _moe(xs, mod_l, norm_ffn_g[layer], router_w[layer], router_b[layer], wgu_bf, moe_b_gu[layer],
                   wd_all, moe_b_down[layer], layer * ne, final_g.astype(F32), nb, n_lat, nrows, last)
        if last:
            out = res
        else:
            xs = res
    return out.reshape(nb, n_lat, d).astype(x.dtype)
```

```python
import functools
import math

import jax
import jax.numpy as jnp
from jax import lax
from jax.experimental import pallas as pl
from jax.experimental.pallas import tpu as pltpu

F32 = jnp.float32
BF16 = jnp.bfloat16

LANES = 128
CHUNK_ROWS = 16
RET_HEADS = 8
RET_DK = 128
RET_CHUNK = 128
ROPE_AXES = (32, 48, 48)
ROPE_THETA = 10000.0
GRID_W = 64
SGU_GROUPS = 8
SGU_CHUNK = 128
TOP_K = 4
MOE_TILE = 256
FETCH_UNROLL = 4
SWIGLU_LIMIT = 7.0
SWIGLU_ALPHA = 1.702
EPS = 1e-6
MOD_GROUPS = 8
HALO = 16
VMEM_CAP = 58 * 2 ** 20


def _cparams(sem, vmem_mb):
    return pltpu.CompilerParams(dimension_semantics=sem,
                                vmem_limit_bytes=min(int(vmem_mb * 2 ** 20), VMEM_CAP))


def _pick_tile(n_lat, n_ctx, cands):
    for t in cands:
        if n_lat % t == 0 and n_ctx % t == 0:
            return t
    raise ValueError("no token tile divides the latent and context lengths")


def _group_of(i, blocks_per_batch, nb):
    return jnp.minimum(i // blocks_per_batch, nb)


def _norm_mod(x, g, shift, scale):
    ms = jnp.mean(x * x, axis=-1, keepdims=True)
    return (x * lax.rsqrt(ms + EPS) * g) * (1.0 + scale) + shift


def _mod_kernel(c_ref, w_ref, b_ref, o_ref):
    c = c_ref[...]
    s = (c * jax.nn.sigmoid(c)).astype(BF16)
    o_ref[0] = jnp.dot(s, w_ref[0].astype(BF16), preferred_element_type=F32) + b_ref[0]


def _mod_table(cc, mod_w, mod_b):
    depth, d, n6 = mod_w.shape
    tn = min(d, 1024)
    return pl.pallas_call(
        _mod_kernel,
        grid=(depth, n6 // tn),
        in_specs=[pl.BlockSpec((MOD_GROUPS, d), lambda l, j: (0, 0)),
                  pl.BlockSpec((1, d, tn), lambda l, j: (l, 0, j)),
                  pl.BlockSpec((1, 1, tn), lambda l, j: (l, 0, j))],
        out_specs=pl.BlockSpec((1, MOD_GROUPS, tn), lambda l, j: (l, 0, j)),
        out_shape=jax.ShapeDtypeStruct((depth, MOD_GROUPS, n6), F32),
        compiler_params=_cparams(("arbitrary", "arbitrary"), 2 * d * tn * 4 / 2 ** 20 + 8),
        name="mod_table",
    )(cc, mod_w, mod_b.reshape(depth, 1, n6))


def _premix_even_kernel(x_ref, mod_ref, g_ref, w_ref, cos_ref, sin_ref, o_ref, h_scr, *, kscale):
    j = pl.program_id(1)

    @pl.when(j == 0)
    def _():
        m = mod_ref[0]
        h_scr[...] = _norm_mod(x_ref[...], g_ref[...], m[0:1], m[1:2]).astype(BF16)

    z = jnp.dot(h_scr[...], w_ref[...], preferred_element_type=F32)

    @pl.when(j < 2)
    def _():
        c = cos_ref[...]
        s = sin_ref[...]
        scale = jnp.where(j == 1, kscale, 1.0).astype(F32)
        for hh in range(z.shape[1] // RET_DK):
            t = z[:, hh * RET_DK:(hh + 1) * RET_DK]
            r = pltpu.roll(t, RET_DK // 2, axis=1)
            o_ref[:, hh * RET_DK:(hh + 1) * RET_DK] = ((t * c + r * s) * scale).astype(BF16)

    @pl.when(j >= 2)
    def _():
        o_ref[...] = z.astype(BF16)


def _premix_even(xs, mod_l, g, w_bf, cos_t, sin_t, nb, n_lat, n_ctx_total):
    nt, d = xs.shape
    n_out = w_bf.shape[1]
    tm = _pick_tile(n_lat, n_ctx_total, (1024, 512, 256))
    tn = RET_HEADS * RET_DK
    bpb = n_lat // tm
    n_lat_blocks = nb * bpb

    def tab_idx(i, j):
        return (jnp.where(i < n_lat_blocks, i % bpb, bpb + (i - n_lat_blocks)), 0)

    return pl.pallas_call(
        functools.partial(_premix_even_kernel, kscale=RET_DK ** -0.5),
        grid=(nt // tm, n_out // tn),
        in_specs=[pl.BlockSpec((tm, d), lambda i, j: (i, 0)),
                  pl.BlockSpec((1, 6, d), lambda i, j: (_group_of(i, bpb, nb), 0, 0)),
                  pl.BlockSpec((1, d), lambda i, j: (0, 0)),
                  pl.BlockSpec((d, tn), lambda i, j: (0, j)),
                  pl.BlockSpec((tm, RET_DK), tab_idx),
                  pl.BlockSpec((tm, RET_DK), tab_idx)],
        out_specs=pl.BlockSpec((tm, tn), lambda i, j: (i, j)),
        out_shape=jax.ShapeDtypeStruct((nt, n_out), BF16),
        scratch_shapes=[pltpu.VMEM((tm, d), BF16)],
        compiler_params=_cparams(("arbitrary", "arbitrary"),
                                 (2 * tm * d * 4 + tm * d * 2 + 2 * d * tn * 2 + 2 * tm * tn * 2
                                  + 2 * tm * tn * 4) / 2 ** 20 + 8),
        name="premix_even",
    )(xs, mod_l, g.reshape(1, d), w_bf, cos_t, sin_t)


def _retention_kernel(lg_ref, q_ref, k_ref, v_ref, g_ref, gn_ref, s0_ref, ret_in_ref, ret_ref, sfin_ref,
                      st_scr, *, nc, unroll):
    del ret_in_ref
    hh = pl.program_id(1)
    lgf = lg_ref[0, hh]
    lgb = lg_ref[1, hh]
    c = RET_CHUNK
    ri = lax.broadcasted_iota(jnp.int32, (c, c), 0).astype(F32)
    ci = lax.broadcasted_iota(jnp.int32, (c, c), 1).astype(F32)
    rel = ri - ci
    dmat = jnp.where(rel >= 0, jnp.exp(lgf * jnp.maximum(rel, 0.0)), jnp.exp(lgb * jnp.maximum(-rel, 0.0)))
    xi_f = jnp.exp(lgf * (ri + 1.0))
    xi_b = jnp.exp(lgb * (c - ri))
    zeta_f = jnp.exp(lgf * (c - 1.0 - ci))
    zeta_b = jnp.exp(lgb * ci)
    gc_f = jnp.exp(lgf * c + jnp.zeros((c, c), F32))
    gc_b = jnp.exp(lgb * c + jnp.zeros((c, c), F32))

    def local_sums(i, carry):
        for u in range(unroll):
            cc = i * unroll + u
            r0 = pl.multiple_of(cc * c, c)
            kt = k_ref[pl.ds(r0, c), :].astype(F32).T
            v = v_ref[pl.ds(r0, c), :]
            st_scr[cc, :, 0:c] = jnp.dot((kt * zeta_f).astype(BF16), v, preferred_element_type=F32)
            st_scr[cc, :, c:2 * c] = jnp.dot((kt * zeta_b).astype(BF16), v, preferred_element_type=F32)
        return carry

    lax.fori_loop(0, nc // unroll, local_sums, 0)

    def scans(i, carry):
        sf, sb = carry
        cb = nc - 1 - i
        uf = st_scr[i, :, 0:c]
        st_scr[i, :, 0:c] = sf
        ub = st_scr[cb, :, c:2 * c]
        st_scr[cb, :, c:2 * c] = sb
        return gc_f * sf + uf, gc_b * sb + ub

    sf, sb = lax.fori_loop(0, nc, scans, (s0_ref[0, 0, 0], s0_ref[0, 0, 1]))
    sfin_ref[0, 0, 0] = sf
    sfin_ref[0, 0, 1] = sb

    def outputs(i, carry):
        for u in range(unroll):
            cc = i * unroll + u
            r0 = pl.multiple_of(cc * c, c)
            q = q_ref[pl.ds(r0, c), :]
            a = lax.dot_general(q, k_ref[pl.ds(r0, c), :], (((1,), (1,)), ((), ())), preferred_element_type=F32)
            o = jnp.dot((a * dmat).astype(BF16), v_ref[pl.ds(r0, c), :], preferred_element_type=F32)
            cr = jnp.dot(q, st_scr[cc].astype(BF16), preferred_element_type=F32)
            o = o + cr[:, 0:c] * xi_f + cr[:, c:2 * c] * xi_b
            mu = jnp.mean(o, axis=-1, keepdims=True)
            dlt = o - mu
            var = jnp.mean(dlt * dlt, axis=-1, keepdims=True)
            y = dlt * lax.rsqrt(var + EPS) * gn_ref[...]
            g = g_ref[pl.ds(r0, c), :].astype(F32)
            ret_ref[pl.ds(r0, c), :] = (g * jax.nn.sigmoid(g) * y).astype(BF16)
        return carry

    lax.fori_loop(0, nc // unroll, outputs, 0)


def _retention(z, lg, gn, s0, ret_prev, nb, seq, row0):
    nt = z.shape[0]
    nc = seq // RET_CHUNK
    rb0 = row0 // seq
    hcols = RET_HEADS

    def col(off):
        return lambda b, h: (rb0 + b, off * hcols + h)

    ret, sfin = pl.pallas_call(
        functools.partial(_retention_kernel, nc=nc, unroll=max(u for u in (16, 8, 4, 2, 1) if nc % u == 0)),
        grid=(nb, RET_HEADS),
        in_specs=[pl.BlockSpec(memory_space=pltpu.SMEM),
                  pl.BlockSpec((seq, RET_DK), col(0)),
                  pl.BlockSpec((seq, RET_DK), col(1)),
                  pl.BlockSpec((seq, RET_DK), col(2)),
                  pl.BlockSpec((seq, RET_DK), col(3)),
                  pl.BlockSpec((1, RET_DK), lambda b, h: (0, h)),
                  pl.BlockSpec((1, 1, 2, RET_DK, RET_DK), lambda b, h: (b, h, 0, 0, 0)),
                  pl.BlockSpec(memory_space=pl.ANY)],
        out_specs=[pl.BlockSpec((seq, RET_DK), lambda b, h: (rb0 + b, h)),
                   pl.BlockSpec((1, 1, 2, RET_DK, RET_DK), lambda b, h: (b, h, 0, 0, 0))],
        out_shape=[jax.ShapeDtypeStruct((nt, RET_HEADS * RET_DK), BF16),
                   jax.ShapeDtypeStruct((nb, RET_HEADS, 2, RET_DK, RET_DK), F32)],
        scratch_shapes=[pltpu.VMEM((nc, RET_DK, 2 * RET_DK), F32)],
        input_output_aliases={7: 0},
        compiler_params=_cparams(("arbitrary", "arbitrary"),
                                 (10 * seq * RET_DK * 2 + 2 * nc * RET_DK * RET_DK * 4) / 2 ** 20 + 8),
        name="retention",
    )(lg, z, z, z, z, gn, s0, ret_prev)
    return ret, sfin


def _conv_kernel(a_ref, b_ref, ap_ref, bp_ref, an_ref, bn_ref, w_ref, cb_ref, lg_ref, lb_ref, o_ref, ext, shifted,
                 *, tm, n_lat_total, n_lat, n_ctx, ktaps):
    i = pl.program_id(0)
    row0 = i * tm
    is_lat = row0 < n_lat_total
    pos = jnp.where(is_lat, row0 % n_lat, (row0 - n_lat_total) % n_ctx)
    seq = jnp.where(is_lat, n_lat, n_ctx)
    keep_prev = (pos != 0).astype(F32)
    keep_next = (pos + tm != seq).astype(F32)

    def glu(a, b):
        return a.astype(F32) * jax.nn.sigmoid(b.astype(F32))

    ext[0:HALO, :] = glu(ap_ref[...], bp_ref[...]) * keep_prev
    ext[HALO:HALO + tm, :] = glu(a_ref[...], b_ref[...])
    ext[HALO + tm:HALO + tm + HALO, :] = glu(an_ref[...], bn_ref[...]) * keep_next

    half = ktaps // 2
    rc = 32
    sub = 8
    span = shifted.shape[1]
    for s in range(1, sub):
        shifted[s - 1] = ext[s:s + span, :]
    for r in range(tm // rc):
        acc = jnp.zeros((rc, a_ref.shape[1]), F32)
        for j in range(ktaps):
            o = HALO + r * rc + j - half
            s, base = o % sub, o - o % sub
            win = ext[base:base + rc, :] if s == 0 else shifted[s - 1, base:base + rc, :]
            acc = acc + w_ref[j:j + 1, :] * win
        y = acc + cb_ref[...]
        mu = jnp.mean(y, axis=-1, keepdims=True)
        dlt = y - mu
        var = jnp.mean(dlt * dlt, axis=-1, keepdims=True)
        yn = dlt * lax.rsqrt(var + EPS) * lg_ref[...] + lb_ref[...]
        o_ref[r * rc:(r + 1) * rc, :] = (yn * jax.nn.sigmoid(yn)).astype(BF16)


def _conv_module(z, conv_w, conv_b, ln_g, ln_b, nb, n_lat, n_ctx):
    nt = z.shape[0]
    ktaps, cw = conv_w.shape
    tm = 256
    assert n_lat % tm == 0 and n_ctx % tm == 0 and ktaps // 2 <= HALO
    a_blk = (2 * RET_HEADS * RET_DK + 2 * RET_HEADS * RET_DK) // cw
    hb = tm // HALO
    last = nt // HALO - 1
    kern = functools.partial(_conv_kernel, tm=tm, n_lat_total=nb * n_lat, n_lat=n_lat, n_ctx=n_ctx, ktaps=ktaps)
    vec = lambda i: (0, 0)
    return pl.pallas_call(
        kern,
        grid=(nt // tm,),
        in_specs=[pl.BlockSpec((tm, cw), lambda i: (i, a_blk)),
                  pl.BlockSpec((tm, cw), lambda i: (i, a_blk + 1)),
                  pl.BlockSpec((HALO, cw), lambda i: (jnp.maximum(i * hb - 1, 0), a_blk)),
                  pl.BlockSpec((HALO, cw), lambda i: (jnp.maximum(i * hb - 1, 0), a_blk + 1)),
                  pl.BlockSpec((HALO, cw), lambda i: (jnp.minimum((i + 1) * hb, last), a_blk)),
                  pl.BlockSpec((HALO, cw), lambda i: (jnp.minimum((i + 1) * hb, last), a_blk + 1)),
                  pl.BlockSpec((ktaps, cw), vec),
                  pl.BlockSpec((1, cw), vec),
                  pl.BlockSpec((1, cw), vec),
                  pl.BlockSpec((1, cw), vec)],
        out_specs=pl.BlockSpec((tm, cw), lambda i: (i, 0)),
        out_shape=jax.ShapeDtypeStruct((nt, cw), BF16),
        scratch_shapes=[pltpu.VMEM((tm + 2 * HALO, cw), F32),
                        pltpu.VMEM((7, tm + 2 * HALO - 8, cw), F32)],
        compiler_params=_cparams(("arbitrary",), 40),
        name="conv_module",
    )(z, z, z, z, z, z, conv_w, conv_b.reshape(1, cw), ln_g.reshape(1, cw), ln_b.reshape(1, cw))


def _outproj_even_kernel(ret_ref, cv_ref, w_ref, x_ref, mod_ref, o_ref):
    rw = ret_ref.shape[1]
    y = jnp.dot(ret_ref[...], w_ref[0:rw, :], preferred_element_type=F32)
    y = y + jnp.dot(cv_ref[...], w_ref[rw:, :], preferred_element_type=F32)
    o_ref[...] = x_ref[...] + mod_ref[0][2:3] * y


def _outproj_even(ret, cv, w_bf, xs, mod_l, nb, n_lat, n_ctx_total):
    nt, d = xs.shape
    tm = _pick_tile(n_lat, n_ctx_total, (512, 256))
    bpb = n_lat // tm
    rw, cw = ret.shape[1], cv.shape[1]
    return pl.pallas_call(
        _outproj_even_kernel,
        grid=(nt // tm,),
        in_specs=[pl.BlockSpec((tm, rw), lambda i: (i, 0)),
                  pl.BlockSpec((tm, cw), lambda i: (i, 0)),
                  pl.BlockSpec((rw + cw, d), lambda i: (0, 0)),
                  pl.BlockSpec((tm, d), lambda i: (i, 0)),
                  pl.BlockSpec((1, 6, d), lambda i: (_group_of(i, bpb, nb), 0, 0))],
        out_specs=pl.BlockSpec((tm, d), lambda i: (i, 0)),
        out_shape=jax.ShapeDtypeStruct((nt, d), F32),
        input_output_aliases={3: 0},
        compiler_params=_cparams(("arbitrary",),
                                 (2 * (rw + cw) * d * 2 + 4 * tm * d * 4 + 2 * tm * (rw + cw) * 2
                                  + tm * d * 4) / 2 ** 20 + 8),
        name="outproj_even",
    )(ret, cv, w_bf, xs, mod_l)


def _premix_odd_kernel(x_ref, mod_ref, g_ref, w_ref, o_ref, h_scr):
    j = pl.program_id(1)

    @pl.when(j == 0)
    def _():
        m = mod_ref[0]
        h_scr[...] = _norm_mod(x_ref[...], g_ref[...], m[0:1], m[1:2]).astype(BF16)

    z = jnp.dot(h_scr[...], w_ref[...], preferred_element_type=F32)
    o_ref[...] = (0.5 * z * (1.0 + lax.erf(z * (2.0 ** -0.5)))).astype(BF16)


def _premix_odd(xs, mod_l, g, w_bf, nb, n_lat, nrows):
    d = xs.shape[1]
    n_out = w_bf.shape[1]
    tm = _pick_tile(n_lat, nrows, (1024, 512, 256))
    tn = 1024
    bpb = n_lat // tm
    return pl.pallas_call(
        _premix_odd_kernel,
        grid=(nrows // tm, n_out // tn),
        in_specs=[pl.BlockSpec((tm, d), lambda i, j: (i, 0)),
                  pl.BlockSpec((1, 6, d), lambda i, j: (_group_of(i, bpb, nb), 0, 0)),
                  pl.BlockSpec((1, d), lambda i, j: (0, 0)),
                  pl.BlockSpec((d, tn), lambda i, j: (0, j))],
        out_specs=pl.BlockSpec((tm, tn), lambda i, j: (i, j)),
        out_shape=jax.ShapeDtypeStruct((nrows, n_out), BF16),
        scratch_shapes=[pltpu.VMEM((tm, d), BF16)],
        compiler_params=_cparams(("arbitrary", "arbitrary"),
                                 (2 * tm * d * 4 + tm * d * 2 + 2 * d * tn * 2 + 2 * tm * tn * 2
                                  + 2 * tm * tn * 4) / 2 ** 20 + 8),
        name="premix_odd",
    )(xs, mod_l, g.reshape(1, d), w_bf)


def _sgu_out_kernel(u_ref, v_ref, lg_ref, lb_ref, ws_ref, bs_ref, w_ref, x_ref, mod_ref, o_ref, p_scr):
    tm, sw = u_ref.shape
    gw = sw // SGU_GROUPS
    v = v_ref[...].astype(F32)
    mu = jnp.mean(v, axis=-1, keepdims=True)
    dlt = v - mu
    var = jnp.mean(dlt * dlt, axis=-1, keepdims=True)
    p_scr[...] = (dlt * lax.rsqrt(var + EPS) * lg_ref[...] + lb_ref[...]).astype(BF16)
    for cc in range(tm // SGU_CHUNK):
        rows = slice(cc * SGU_CHUNK, (cc + 1) * SGU_CHUNK)
        for gi in range(SGU_GROUPS):
            cols = slice(gi * gw, (gi + 1) * gw)
            mixed = jnp.dot(ws_ref[gi], p_scr[rows, cols], preferred_element_type=F32) + bs_ref[:, cols]
            p_scr[rows, cols] = (u_ref[rows, cols].astype(F32) * mixed).astype(BF16)
    y = jnp.dot(p_scr[...], w_ref[...], preferred_element_type=F32)
    o_ref[...] = x_ref[...] + mod_ref[0][2:3] * y


def _sgu_out(uv, ln_g, ln_b, ws_bf, bias_full, w_bf, xs, mod_l, nb, n_lat, nrows):
    nt, d = xs.shape
    sw = w_bf.shape[0]
    tm = _pick_tile(n_lat, nrows, (512, 256))
    bpb = n_lat // tm
    vec = lambda i: (0, 0)
    return pl.pallas_call(
        _sgu_out_kernel,
        grid=(nrows // tm,),
        in_specs=[pl.BlockSpec((tm, sw), lambda i: (i, 0)),
                  pl.BlockSpec((tm, sw), lambda i: (i, 1)),
                  pl.BlockSpec((1, sw), vec),
                  pl.BlockSpec((1, sw), vec),
                  pl.BlockSpec((SGU_GROUPS, SGU_CHUNK, SGU_CHUNK), lambda i: (0, 0, 0)),
                  pl.BlockSpec((SGU_CHUNK, sw), vec),
                  pl.BlockSpec((sw, d), vec),
                  pl.BlockSpec((tm, d), lambda i: (i, 0)),
                  pl.BlockSpec((1, 6, d), lambda i: (_group_of(i, bpb, nb), 0, 0))],
        out_specs=pl.BlockSpec((tm, d), lambda i: (i, 0)),
        out_shape=jax.ShapeDtypeStruct((nt, d), F32),
        scratch_shapes=[pltpu.VMEM((tm, sw), BF16)],
        input_output_aliases={7: 0},
        compiler_params=_cparams(("arbitrary",),
                                 (2 * sw * d * 2 + 4 * tm * d * 4 + 4 * tm * sw * 2 + tm * sw * 2
                                  + 2 * SGU_CHUNK * sw * 4 + 2 * tm * sw * 4) / 2 ** 20 + 8),
        name="sgu_out",
    )(uv, uv, ln_g.reshape(1, sw), ln_b.reshape(1, sw), ws_bf, bias_full, w_bf, xs, mod_l)


def _router_kernel(x_ref, mod_ref, g_ref, w_ref, rb_ref, h_ref, aux_ref, cnt_ref, before_ref, carry):
    i = pl.program_id(0)
    tm = x_ref.shape[0]

    @pl.when(i == 0)
    def _():
        carry[...] = jnp.zeros_like(carry)

    m = mod_ref[0]
    h = _norm_mod(x_ref[...], g_ref[...], m[3:4], m[4:5])
    h_ref[...] = h
    h_hi = h.astype(BF16)
    h_lo = (h - h_hi.astype(F32)).astype(BF16)
    p_hi = jnp.dot(h_hi, w_ref[...], preferred_element_type=F32)
    p_lo = jnp.dot(h_lo, w_ref[:, 0:LANES], preferred_element_type=F32)
    logits = p_hi[:, 0:LANES] + p_lo + p_hi[:, LANES:2 * LANES] + rb_ref[...]

    lane = lax.broadcasted_iota(jnp.int32, (tm, LANES), 1).astype(F32)
    vals, idxs = [], []
    l = logits
    for _ in range(TOP_K):
        mx = jnp.max(l, axis=-1, keepdims=True)
        ix = jnp.min(jnp.where(l == mx, lane, float(LANES)), axis=-1, keepdims=True)
        vals.append(mx)
        idxs.append(ix)
        l = jnp.where(lane == ix, -jnp.inf, l)
    es = [jnp.exp(v - vals[0]) for v in vals]
    den = es[0]
    for e in es[1:]:
        den = den + e

    sel = jnp.zeros((tm, LANES), F32)
    for ix in idxs:
        sel = sel + jnp.where(lane == ix, 1.0, 0.0)
    rr = lax.broadcasted_iota(jnp.int32, (tm, tm), 0)
    rc = lax.broadcasted_iota(jnp.int32, (tm, tm), 1)
    ltri = jnp.where(rc < rr, 1.0, 0.0).astype(BF16)
    rank = jnp.dot(ltri, sel.astype(BF16), preferred_element_type=F32) + carry[...]
    before_ref[0] = jnp.broadcast_to(carry[...], before_ref.shape[1:])
    carry[...] = carry[...] + jnp.sum(sel, axis=0, keepdims=True)
    cnt_ref[...] = carry[...]

    aux = jnp.zeros((tm, LANES), F32)
    for r in range(TOP_K):
        rk = jnp.sum(jnp.where(lane == idxs[r], rank, 0.0), axis=-1, keepdims=True)
        aux = jnp.where(lane == float(r), idxs[r], aux)
        aux = jnp.where(lane == float(TOP_K + r), rk, aux)
        aux = jnp.where(lane == float(2 * TOP_K + r), es[r] / den, aux)
    aux_ref[...] = aux


def _router(xs, mod_l, g, w_hilo, rb, nb, n_lat, nrows):
    d = xs.shape[1]
    tm = MOE_TILE
    bpb = n_lat // tm
    vec = lambda i: (0, 0)
    return pl.pallas_call(
        _router_kernel,
        grid=(nrows // tm,),
        in_specs=[pl.BlockSpec((tm, d), lambda i: (i, 0)),
                  pl.BlockSpec((1, 6, d), lambda i: (_group_of(i, bpb, nb), 0, 0)),
                  pl.BlockSpec((1, d), vec),
                  pl.BlockSpec((d, 2 * LANES), vec),
                  pl.BlockSpec((1, LANES), vec)],
        out_specs=[pl.BlockSpec((tm, d), lambda i: (i, 0)),
                   pl.BlockSpec((tm, LANES), lambda i: (i, 0)),
                   pl.BlockSpec((1, LANES), vec),
                   pl.BlockSpec((1, 8, LANES), lambda i: (i, 0, 0))],
        out_shape=[jax.ShapeDtypeStruct((nrows, d), F32),
                   jax.ShapeDtypeStruct((nrows, LANES), F32),
                   jax.ShapeDtypeStruct((1, LANES), F32),
                   jax.ShapeDtypeStruct((nrows // tm, 8, LANES), F32)],
        scratch_shapes=[pltpu.VMEM((1, LANES), F32)],
        compiler_params=_cparams(("arbitrary",), 32),
        name="router",
    )(xs, mod_l, g.reshape(1, d), w_hilo, rb)


def _dispatch_kernel(pos_ref, h_ref, hs_ref, sem):
    tm = h_ref.shape[0]

    def row_copy(t, p):
        return pltpu.make_async_copy(h_ref.at[pl.ds(t, 1), :], hs_ref.at[pl.ds(p, 1), :], sem)

    def issue(t, carry):
        for r in range(TOP_K):
            row_copy(t, pos_ref[0, 0, t * TOP_K + r]).start(priority=r % 2)
        return carry

    lax.fori_loop(0, tm, issue, 0)
    for _ in range(TOP_K):
        pltpu.make_async_copy(h_ref, hs_ref.at[pl.ds(0, tm), :], sem).wait()


def _dispatch(h, pos):
    nrows, d = h.shape
    tm = MOE_TILE
    pos3 = pos.reshape(nrows // tm, 1, tm * TOP_K)
    return pl.pallas_call(
        _dispatch_kernel,
        grid=(nrows // tm,),
        in_specs=[pl.BlockSpec((1, 1, tm * TOP_K), lambda i: (i, 0, 0), memory_space=pltpu.SMEM),
                  pl.BlockSpec((tm, d), lambda i: (i, 0))],
        out_specs=pl.BlockSpec(memory_space=pl.ANY),
        out_shape=jax.ShapeDtypeStruct((nrows * TOP_K, d), F32),
        scratch_shapes=[pltpu.SemaphoreType.DMA(())],
        compiler_params=_cparams(("arbitrary",), 16),
        name="moe_dispatch",
    )(pos3, h)


def _pair_perm():
    i = jnp.arange(2 * LANES)[:, None]
    j = jnp.arange(2 * LANES)[None, :]
    return jnp.where(j < LANES, i == 2 * j, i == 2 * (j - LANES) + 1).astype(BF16)


def _prep_wgu_kernel(w_ref, p_ref, o_ref):
    blk = 2 * LANES
    for b in range(w_ref.shape[2] // blk):
        cols = slice(b * blk, (b + 1) * blk)
        o_ref[0, :, cols] = jnp.dot(w_ref[0, :, cols].astype(BF16), p_ref[...],
                                    preferred_element_type=F32).astype(BF16)


def _prep_wgu(wgu):
    lead = wgu.shape[:-2]
    d, n2 = wgu.shape[-2:]
    ne = math.prod(lead)
    tr = 1024 if d % 1024 == 0 else d
    out = pl.pallas_call(
        _prep_wgu_kernel,
        grid=(ne, d // tr),
        in_specs=[pl.BlockSpec((1, tr, n2), lambda e, r: (e, r, 0)),
                  pl.BlockSpec((2 * LANES, 2 * LANES), lambda e, r: (0, 0))],
        out_specs=pl.BlockSpec((1, tr, n2), lambda e, r: (e, r, 0)),
        out_shape=jax.ShapeDtypeStruct((ne, d, n2), BF16),
        compiler_params=_cparams(("arbitrary", "arbitrary"), 3 * tr * n2 * 4 / 2 ** 20 + 8),
        name="prep_wgu",
    )(wgu.reshape(ne, d, n2), _pair_perm())
    return out.reshape(lead + (d, n2))


def _expert_kernel(tile_ref, exp_ref, lo_ref, hi_ref, first_ref, hs_ref, wgu_ref, bgu_ref, wd_ref, bd_ref, o_ref):
    del tile_ref, exp_ref
    w = pl.program_id(0)
    tm = hs_ref.shape[0]
    lo, hi, first = lo_ref[w], hi_ref[w], first_ref[w]

    def rows_block(r0, nr):
        rows = slice(r0, r0 + nr)
        z = jnp.dot(hs_ref[rows, :].astype(BF16), wgu_ref[0], preferred_element_type=F32) + bgu_ref[0]
        parts = []
        for b in range(z.shape[1] // (2 * LANES)):
            zg = jnp.minimum(z[:, 2 * b * LANES:(2 * b + 1) * LANES], SWIGLU_LIMIT)
            zu = jnp.clip(z[:, (2 * b + 1) * LANES:(2 * b + 2) * LANES], -SWIGLU_LIMIT, SWIGLU_LIMIT)
            parts.append(((zu + 1.0) * (zg * jax.nn.sigmoid(SWIGLU_ALPHA * zg))).astype(BF16))
        act = jnp.concatenate(parts, axis=1)
        y = jnp.dot(act, wd_ref[0].astype(BF16), preferred_element_type=F32) + bd_ref[0]
        y = y.astype(o_ref.dtype)
        row = r0 + lax.broadcasted_iota(jnp.int32, (nr, 1), 0)
        mine = (row >= lo) & (row < hi)

        @pl.when(first == 1)
        def _():
            o_ref[rows, :] = jnp.where(mine, y, jnp.zeros_like(y))

        @pl.when(first == 0)
        def _():
            o_ref[rows, :] = jnp.where(mine, y, o_ref[rows, :])

    whole = (lo == 0) & (hi == tm)

    @pl.when(whole)
    def _():
        rows_block(0, tm)

    @pl.when(jnp.logical_not(whole))
    def _():
        half = tm // 2
        for r0 in (0, half):
            touched = (lo < r0 + half) & (hi > r0)

            @pl.when(touched)
            def _():
                rows_block(r0, half)

            @pl.when(jnp.logical_not(touched) & (first == 1))
            def _():
                o_ref[r0:r0 + half, :] = jnp.zeros((half, o_ref.shape[1]), o_ref.dtype)


def _experts(hs, items, wgu_bf, bgu, wd_all, bd, tm, ebase):
    p, d = hs.shape
    n2 = wgu_bf.shape[2]
    ne = bgu.shape[0]
    de = n2 // 2
    nw = items[0].shape[0]
    grid_spec = pltpu.PrefetchScalarGridSpec(
        num_scalar_prefetch=5,
        grid=(nw,),
        in_specs=[pl.BlockSpec((tm, d), lambda w, t, e, lo, hi, f: (t[w], 0)),
                  pl.BlockSpec((1, d, n2), lambda w, t, e, lo, hi, f: (ebase + e[w], 0, 0)),
                  pl.BlockSpec((1, 1, n2), lambda w, t, e, lo, hi, f: (e[w], 0, 0)),
                  pl.BlockSpec((1, de, d), lambda w, t, e, lo, hi, f: (ebase + e[w], 0, 0)),
                  pl.BlockSpec((1, 1, d), lambda w, t, e, lo, hi, f: (e[w], 0, 0))],
        out_specs=pl.BlockSpec((tm, d), lambda w, t, e, lo, hi, f: (t[w], 0)),
    )
    return pl.pallas_call(
        _expert_kernel,
        grid_spec=grid_spec,
        out_shape=jax.ShapeDtypeStruct((p, d), BF16),
        compiler_params=_cparams(("arbitrary",),
                                 (2 * tm * d * 4 + 2 * tm * d * 2 + 2 * d * n2 * 2 + 2 * de * d * 4 + de * d * 2
                                  + 2 * tm * n2 * 4 + tm * d * 4) / 2 ** 20 + 6),
        name="moe_experts",
    )(*items, hs, wgu_bf, bgu.reshape(ne, 1, n2), wd_all, bd.reshape(ne, 1, d))


def _expert_items(counts, p, tm, ne):
    ntiles = p // tm
    nw = ntiles + ne - 1
    off = jnp.concatenate([jnp.zeros((1,), jnp.int32), jnp.cumsum(counts)])
    first_tile = off[:-1] // tm
    last_tile = (off[1:] - 1) // tm
    ntile_e = jnp.where(counts > 0, last_tile - first_tile + 1, 0)
    base_incl = jnp.cumsum(ntile_e)
    base = base_incl - ntile_e
    total = base_incl[-1]
    w = jnp.arange(nw, dtype=jnp.int32)
    wc = jnp.minimum(w, total - 1)
    e = jnp.sum((base_incl[None, :] <= wc[:, None]).astype(jnp.int32), axis=1)
    t = first_tile[e] + (wc - base[e])
    lo = jnp.clip(off[e] - t * tm, 0, tm)
    hi = jnp.clip(off[e + 1] - t * tm, 0, tm)
    valid = w < total
    lo = jnp.where(valid, lo, 0)
    hi = jnp.where(valid, hi, 0)
    prev_t = jnp.concatenate([jnp.full((1,), -1, jnp.int32), t[:-1]])
    first = (valid & (t != prev_t)).astype(jnp.int32)
    return (t.astype(jnp.int32), e, lo.astype(jnp.int32), hi.astype(jnp.int32), first), off


def _combine_kernel(src_ref, tot_ref, ys_ref, aux_ref, tab_ref, x_ref, mod_ref, fg_ref, o_ref,
                    buf, sem, *, final, ntiles):
    i = pl.program_id(0)
    tm = x_ref.shape[0]
    nrow = buf.shape[1]
    slot = i % 2

    def fetch(tile, sl):
        def per_group(gi, c):
            for u in range(FETCH_UNROLL):
                s = gi * FETCH_UNROLL + u
                src_row = pl.multiple_of(src_ref[tile * (nrow // CHUNK_ROWS) + s] * CHUNK_ROWS, CHUNK_ROWS)
                dst_row = pl.multiple_of(s * CHUNK_ROWS, CHUNK_ROWS)
                pltpu.make_async_copy(ys_ref.at[pl.ds(src_row, CHUNK_ROWS), :],
                                      buf.at[sl, pl.ds(dst_row, CHUNK_ROWS), :], sem.at[sl]).start()
            return c
        lax.fori_loop(0, tot_ref[tile] // FETCH_UNROLL, per_group, 0)

    def wait_fetched(tile, sl):
        total = tot_ref[tile]
        for b in range((nrow // CHUNK_ROWS).bit_length()):
            rows = CHUNK_ROWS << b

            @pl.when((total >> b) & 1 == 1)
            def _():
                pltpu.make_async_copy(ys_ref.at[pl.ds(0, rows), :], buf.at[sl, pl.ds(0, rows), :],
                                      sem.at[sl]).wait()

    @pl.when(i == 0)
    def _():
        buf[...] = jnp.zeros_like(buf)
        fetch(0, 0)

    @pl.when(i + 1 < ntiles)
    def _():
        fetch(i + 1, 1 - slot)

    wait_fetched(i, slot)

    aux = aux_ref[...]
    tab = tab_ref[0]
    lane = lax.broadcasted_iota(jnp.int32, (tm, LANES), 1).astype(F32)
    col = lax.broadcasted_iota(jnp.int32, (tm, nrow), 1).astype(F32)
    gmat = jnp.zeros((tm, nrow), F32)
    for r in range(TOP_K):
        base = jnp.sum(jnp.where(lane == aux[:, r:r + 1], tab, 0.0), axis=-1, keepdims=True)
        lrow = base + aux[:, TOP_K + r:TOP_K + r + 1]
        gmat = gmat + jnp.where(col == lrow, aux[:, 2 * TOP_K + r:2 * TOP_K + r + 1], 0.0)
    y = jnp.dot(gmat.astype(BF16), buf[slot], preferred_element_type=F32)
    xn = x_ref[...] + mod_ref[0][5:6] * y
    if final:
        ms = jnp.mean(xn * xn, axis=-1, keepdims=True)
        xn = xn * lax.rsqrt(ms + EPS) * fg_ref[...]
    o_ref[...] = xn


def _combine_plan(off, before, counts, ne, max_chunks):
    start = off[None, :ne] + before
    end = off[None, :ne] + jnp.concatenate([before[1:], counts[None, :]], axis=0)
    c0 = start // CHUNK_ROWS
    cc = jnp.where(end > start, (end + CHUNK_ROWS - 1) // CHUNK_ROWS - c0, 0)
    sb_end = jnp.cumsum(cc, axis=1)
    sb = sb_end - cc
    tab = (sb - c0) * CHUNK_ROWS + off[None, :ne]
    tab = jnp.zeros((before.shape[0], 1, LANES), F32).at[:, 0, :ne].set(tab.astype(F32))
    slot = jnp.arange(max_chunks, dtype=jnp.int32)[None, :, None]
    owner = (sb[:, None, :] <= slot) & (slot < sb_end[:, None, :])
    src = jnp.sum(jnp.where(owner, (c0 - sb)[:, None, :], 0), axis=-1) + slot[:, :, 0]
    flat = lambda a: a.reshape(-1).astype(jnp.int32)
    tot = (sb_end[:, -1] + FETCH_UNROLL - 1) // FETCH_UNROLL * FETCH_UNROLL
    return flat(src), flat(tot), tab


def _combine(ys, plan, aux, xs, mod_l, final_g, nb, n_lat, nrows, final):
    d = xs.shape[1]
    tm = MOE_TILE
    bpb = n_lat // tm
    ntiles = nrows // tm
    src, tot, tab = plan
    max_chunks = src.shape[0] // ntiles
    out_rows = nrows if final else xs.shape[0]
    grid_spec = pltpu.PrefetchScalarGridSpec(
        num_scalar_prefetch=2,
        grid=(ntiles,),
        in_specs=[pl.BlockSpec(memory_space=pl.ANY),
                  pl.BlockSpec((tm, LANES), lambda i, *_: (i, 0)),
                  pl.BlockSpec((1, 1, LANES), lambda i, *_: (i, 0, 0)),
                  pl.BlockSpec((tm, d), lambda i, *_: (i, 0)),
                  pl.BlockSpec((1, 6, d), lambda i, *_: (_group_of(i, bpb, nb), 0, 0)),
                  pl.BlockSpec((1, d), lambda i, *_: (0, 0))],
        out_specs=pl.BlockSpec((tm, d), lambda i, *_: (i, 0)),
        scratch_shapes=[pltpu.VMEM((2, max_chunks * CHUNK_ROWS, d), ys.dtype), pltpu.SemaphoreType.DMA((2,))],
    )
    return pl.pallas_call(
        functools.partial(_combine_kernel, final=final, ntiles=ntiles),
        grid_spec=grid_spec,
        out_shape=jax.ShapeDtypeStruct((out_rows, d), F32),
        input_output_aliases={} if final else {5: 0},
        compiler_params=_cparams(("arbitrary",),
                                 (2 * max_chunks * CHUNK_ROWS * d * 2 + 4 * tm * max_chunks * CHUNK_ROWS * 4 + 8 * tm * d * 4) / 2 ** 20 + 6),
        name="moe_combine",
    )(src, tot, ys, aux, tab, xs, mod_l, final_g.reshape(1, d))


def _moe(xs, mod_l, g, rw, rb, wgu_bf, bgu, wd_all, bd, ebase, final_g, nb, n_lat, nrows, final):
    d = xs.shape[1]
    ne = rw.shape[1]
    rw_pad = jnp.zeros((d, LANES), F32).at[:, :ne].set(rw.astype(F32))
    whi = rw_pad.astype(BF16)
    wlo = (rw_pad - whi.astype(F32)).astype(BF16)
    rb_pad = jnp.full((1, LANES), -1e30, F32).at[0, :ne].set(rb.astype(F32))
    h, aux, cnt, before = _router(xs, mod_l, g, jnp.concatenate([whi, wlo], axis=1), rb_pad, nb, n_lat, nrows)

    counts = cnt[0, :ne].astype(jnp.int32)
    before = before[:, 0, :ne].astype(jnp.int32)
    tm_e = 512
    p = nrows * TOP_K
    items, off = _expert_items(counts, p, tm_e, ne)
    idx = aux[:, 0:TOP_K].astype(jnp.int32)
    rank = aux[:, TOP_K:2 * TOP_K].astype(jnp.int32)
    pos = (off[idx] + rank).reshape(-1)

    hs = _dispatch(h, pos)
    n2 = bgu.shape[1]
    bgu_p = bgu.astype(F32).reshape(ne, n2 // (2 * LANES), LANES, 2).transpose(0, 1, 3, 2).reshape(ne, n2)
    ys = _experts(hs, items, wgu_bf, bgu_p, wd_all, bd.astype(F32), tm_e, ebase)
    max_chunks = MOE_TILE * TOP_K // CHUNK_ROWS + 2 * ne
    assert max_chunks % FETCH_UNROLL == 0
    plan = _combine_plan(off, before, counts, ne, max_chunks)
    return _combine(ys, plan, aux, xs, mod_l, final_g, nb, n_lat, nrows, final)


def _rope_tables(nb, n_lat, n_ctx):
    def angles(seq_pos, row_pos, col_pos):
        parts = []
        for posv, dim in zip((seq_pos, row_pos, col_pos), ROPE_AXES):
            inv = ROPE_THETA ** (-jnp.arange(0, dim, 2, dtype=F32) / dim)
            parts.append(posv.astype(F32)[:, None] * inv[None, :])
        return jnp.concatenate(parts, axis=-1)

    rows = n_lat // GRID_W
    lat = angles(jnp.full((n_lat,), n_ctx, jnp.int32),
                 jnp.repeat(jnp.arange(rows, dtype=jnp.int32), GRID_W),
                 jnp.tile(jnp.arange(GRID_W, dtype=jnp.int32), rows))
    zl = jnp.zeros((n_ctx,), jnp.int32)
    ctx = angles(jnp.arange(n_ctx, dtype=jnp.int32), zl, zl)
    ang = jnp.concatenate([lat, jnp.tile(ctx, (nb, 1))], axis=0)
    cos, sin = jnp.cos(ang), jnp.sin(ang)
    return jnp.concatenate([cos, cos], axis=-1), jnp.concatenate([-sin, sin], axis=-1)


def _deinterleave_heads(w):
    d, n = w.shape
    w4 = w.reshape(d, n // RET_DK, RET_DK // 2, 2)
    return jnp.concatenate([w4[..., 0], w4[..., 1]], axis=-1).reshape(d, n)


def kernel(x, c, ctx, c_ctx, mod_w, mod_b, norm_mix_g, norm_ffn_g, ev_w_in, ev_w_out, ret_decay_f, ret_decay_b,
           ret_gn_g, conv_w, conv_b, conv_ln_g, conv_ln_b, od_w_in, od_w_out, sgu_ln_g, sgu_ln_b, sgu_w, sgu_b,
           router_w, router_b, moe_w_gu, moe_b_gu, moe_w_down, moe_b_down, final_g):
    nb, n_lat, d = x.shape
    n_ctx = ctx.shape[1]
    depth = mod_w.shape[0]
    n_lat_total = nb * n_lat
    n_ctx_total = nb * n_ctx
    nt = n_lat_total + n_ctx_total
    assert nb < MOD_GROUPS and n_lat % GRID_W == 0
    qk = RET_HEADS * RET_DK

    xs = jnp.concatenate([x.reshape(n_lat_total, d), ctx.reshape(n_ctx_total, d)], axis=0).astype(F32)
    cc = jnp.zeros((MOD_GROUPS, d), F32).at[:nb].set(c.astype(F32)).at[nb].set(c_ctx.astype(F32))
    mod = _mod_table(cc, mod_w, mod_b).reshape(depth, MOD_GROUPS, 6, d)
    cos_t, sin_t = _rope_tables(nb, n_lat, n_ctx)
    ne = router_w.shape[2]
    wgu_bf = _prep_wgu(moe_w_gu.reshape((depth * ne,) + moe_w_gu.shape[2:]))
    wd_all = moe_w_down.reshape((depth * ne,) + moe_w_down.shape[2:])

    out = None
    for layer in range(depth):
        last = layer == depth - 1
        nrows = n_lat_total if last else nt
        mod_l = mod[layer]
        if layer % 2 == 0:
            e = layer // 2
            w_in = ev_w_in[e]
            w_in = jnp.concatenate([_deinterleave_heads(w_in[:, :qk]), _deinterleave_heads(w_in[:, qk:2 * qk]),
                                    w_in[:, 2 * qk:]], axis=1).astype(BF16)
            z = _premix_even(xs, mod_l, norm_mix_g[layer], w_in, cos_t, sin_t, nb, n_lat, n_ctx_total)
            lg = jnp.stack([-jnp.exp(ret_decay_f[e].astype(F32)), -jnp.exp(ret_decay_b[e].astype(F32))])
            gn = ret_gn_g[e].reshape(1, qk).astype(F32)
            s0 = jnp.zeros((nb, RET_HEADS, 2, RET_DK, RET_DK), F32)
            ret0 = jnp.zeros((nt, qk), BF16)
            ret, st = _retention(z, lg, gn, s0, ret0, nb, n_ctx, n_lat_total)
            ret, _ = _retention(z, lg, gn, st, ret, nb, n_lat, 0)
            cv = _conv_module(z, conv_w[e].astype(F32), conv_b[e].astype(F32), conv_ln_g[e].astype(F32),
                              conv_ln_b[e].astype(F32), nb, n_lat, n_ctx)
            xs = _outproj_even(ret, cv, ev_w_out[e].astype(BF16), xs, mod_l, nb, n_lat, n_ctx_total)
        else:
            j = layer // 2
            uv = _premix_odd(xs, mod_l, norm_mix_g[layer], od_w_in[j].astype(BF16), nb, n_lat, nrows)
            sw = od_w_out.shape[1]
            bias_full = jnp.repeat(sgu_b[j].astype(F32).T, sw // SGU_GROUPS, axis=1)
            xs = _sgu_out(uv, sgu_ln_g[j].astype(F32), sgu_ln_b[j].astype(F32), sgu_w[j].astype(BF16), bias_full,
                          od_w_out[j].astype(BF16), xs, mod_l, nb, n_lat, nrows)
        res = _moe(xs, mod_l, norm_ffn_g[layer], router_w[layer], router_b[layer], wgu_bf, moe_b_gu[layer],
                   wd_all, moe_b_down[layer], layer * ne, final_g.astype(F32), nb, n_lat, nrows, last)
        if last:
            out = res
        else:
            xs = res
    return out.reshape(nb, n_lat, d).astype(x.dtype)
```
